```python
import jax, jax.numpy as jnp
from jax import lax
import numpy as np

D_MODEL = 2048
BATCH = 2
SEQ = 4096
DEPTH = 1

D_MIX = D_MODEL
GMLP_WIDTH = D_MIX // 2
GMLP_GROUP = 128
GMLP_HEADS = GMLP_WIDTH // GMLP_GROUP
CHUNK = 128
V_HEAD = 128
MLA_WIDTH = D_MIX - GMLP_WIDTH
MLA_HEADS = MLA_WIDTH // V_HEAD
QK_NOPE = 128
QK_ROPE = 64
Q_RANK = D_MODEL // 4
KV_RANK = D_MODEL // 8
IN_COLS = 2 * GMLP_WIDTH + Q_RANK + KV_RANK + QK_ROPE
D_FF = ((8 * D_MODEL // 3 + 255) // 256) * 256
Q_BLOCK = 128
ROPE_THETA = 10000.0
EPS = 1e-6
N_MOD = 9

kernel_name = "hybrid_gmlp_mla_macaron_adaln_encoder"


def _rms(x, g):
    xf = x.astype(jnp.float32)
    y = xf * lax.rsqrt(jnp.mean(xf * xf, axis=-1, keepdims=True) + EPS)
    return (y * g.astype(jnp.float32)).astype(x.dtype)


def _modulate(x, shift, scale):
    return x * (1 + scale[:, None, :]) + shift[:, None, :]


def _swiglu(x, w_gate, w_up, w_down):
    return (jax.nn.silu(x @ w_gate) * (x @ w_up)) @ w_down


def _rotate(x, cos, sin):
    x1, x2 = jnp.split(x, 2, axis=-1)
    return jnp.concatenate([x1 * cos - x2 * sin, x1 * sin + x2 * cos], axis=-1)


def _gmlp_mixer(u, v, v_norm, w_s, b_s):
    B, S, _ = u.shape
    u = jax.nn.gelu(u)
    v = _rms(jax.nn.gelu(v), v_norm)
    v = v.reshape(B, S // CHUNK, CHUNK, GMLP_HEADS, GMLP_GROUP)
    mixed = jnp.einsum('hpq,bnqhc->bnphc', w_s, v) + b_s.T[None, None, :, :, None]
    return u * mixed.reshape(B, S, GMLP_WIDTH)


def _mla_mixer(q_lat, kv_lat, k_pe, cos, sin, q_lat_norm, w_uq, kv_lat_norm, w_ukv,
               q_nope_norm, q_rope_norm, k_nope_norm, k_rope_norm):
    B, S, _ = q_lat.shape
    q = (_rms(q_lat, q_lat_norm) @ w_uq).reshape(B, S, MLA_HEADS, QK_NOPE + QK_ROPE)
    q_nope = _rms(q[..., :QK_NOPE], q_nope_norm)
    q_pe = _rotate(_rms(q[..., QK_NOPE:], q_rope_norm),
                   cos[:, :, None, :], sin[:, :, None, :])
    kv = (_rms(kv_lat, kv_lat_norm) @ w_ukv).reshape(B, S, MLA_HEADS, QK_NOPE + V_HEAD)
    k_nope = _rms(kv[..., :QK_NOPE], k_nope_norm)
    v = kv[..., QK_NOPE:]
    k_pe = _rotate(_rms(k_pe, k_rope_norm), cos, sin)
    scale = (QK_NOPE + QK_ROPE) ** -0.5
    nb = S // Q_BLOCK
    qn_b = q_nope.reshape(B, nb, Q_BLOCK, MLA_HEADS, QK_NOPE).transpose(1, 0, 2, 3, 4)
    qp_b = q_pe.reshape(B, nb, Q_BLOCK, MLA_HEADS, QK_ROPE).transpose(1, 0, 2, 3, 4)

    def block(args):
        qn, qp = args
        s = (jnp.einsum('bqhd,bkhd->bhqk', qn, k_nope)
             + jnp.einsum('bqhd,bkd->bhqk', qp, k_pe))
        p = jax.nn.softmax(s.astype(jnp.float32) * scale, axis=-1).astype(v.dtype)
        return jnp.einsum('bhqk,bkhd->bqhd', p, v)

    o = lax.map(block, (qn_b, qp_b))
    return o.transpose(1, 0, 2, 3, 4).reshape(B, S, MLA_WIDTH)


def setup_inputs(seed: int = 0) -> dict:
    key = jax.random.key(seed)
    ks = iter(jax.random.split(key, 40))

    def nrm(shape, scale):
        return jax.random.normal(next(ks), shape, jnp.float32) * scale

    def gain(shape):
        return 1.0 + 0.02 * jax.random.normal(next(ks), shape, jnp.float32)

    L, D = DEPTH, D_MODEL
    positions = jnp.tile(jnp.arange(SEQ, dtype=jnp.int32)[None, :], (BATCH, 1))
    return {
        "x": nrm((BATCH, SEQ, D), 1.0),
        "c": nrm((BATCH, D), 1.0),
        "positions": positions,
        "w_ada": nrm((L, D, N_MOD * D), 0.5 * D ** -0.5),
        "b_ada": nrm((L, N_MOD * D), 0.01),
        "ffn1_norm": gain((L, D)),
        "ffn1_w_gate": nrm((L, D, D_FF), D ** -0.5),
        "ffn1_w_up": nrm((L, D, D_FF), D ** -0.5),
        "ffn1_w_down": nrm((L, D_FF, D), D_FF ** -0.5),
        "mix_norm": gain((L, D)),
        "w_in": nrm((L, D, IN_COLS), D ** -0.5),
        "gmlp_v_norm": gain((L, GMLP_WIDTH)),
        "gmlp_w_s": nrm((L, GMLP_HEADS, CHUNK, CHUNK), CHUNK ** -0.5),
        "gmlp_b_s": gain((L, GMLP_HEADS, CHUNK)),
        "q_lat_norm": gain((L, Q_RANK)),
        "w_uq": nrm((L, Q_RANK, MLA_HEADS * (QK_NOPE + QK_ROPE)), Q_RANK ** -0.5),
        "kv_lat_norm": gain((L, KV_RANK)),
        "w_ukv": nrm((L, KV_RANK, MLA_HEADS * (QK_NOPE + V_HEAD)), KV_RANK ** -0.5),
        "q_nope_norm": gain((L, QK_NOPE)),
        "q_rope_norm": gain((L, QK_ROPE)),
        "k_nope_norm": gain((L, QK_NOPE)),
        "k_rope_norm": gain((L, QK_ROPE)),
        "out_norm_gmlp": gain((L, GMLP_WIDTH)),
        "out_norm_mla": gain((L, MLA_WIDTH)),
        "w_out": nrm((L, D_MIX, D), D_MIX ** -0.5),
        "ffn2_norm": gain((L, D)),
        "ffn2_w_gate": nrm((L, D, D_FF), D ** -0.5),
        "ffn2_w_up": nrm((L, D, D_FF), D ** -0.5),
        "ffn2_w_down": nrm((L, D_FF, D), D_FF ** -0.5),
        "final_norm": gain((L, D)),
    }


def reference(x, c, positions, w_ada, b_ada, ffn1_norm, ffn1_w_gate, ffn1_w_up, ffn1_w_down,
              mix_norm, w_in, gmlp_v_norm, gmlp_w_s, gmlp_b_s, q_lat_norm, w_uq,
              kv_lat_norm, w_ukv, q_nope_norm, q_rope_norm, k_nope_norm, k_rope_norm,
              out_norm_gmlp, out_norm_mla, w_out, ffn2_norm, ffn2_w_gate, ffn2_w_up,
              ffn2_w_down, final_norm):
    inv_freq = ROPE_THETA ** (-jnp.arange(0, QK_ROPE, 2, dtype=jnp.float32) / QK_ROPE)
    ang = positions.astype(jnp.float32)[..., None] * inv_freq
    cos = jnp.cos(ang).astype(x.dtype)
    sin = jnp.sin(ang).astype(x.dtype)
    c_act = jax.nn.silu(c)
    o1 = 2 * GMLP_WIDTH
    o2 = o1 + Q_RANK
    o3 = o2 + KV_RANK

    h = x
    for l in range(DEPTH):
        mod = c_act @ w_ada[l] + b_ada[l]
        sh1, sc1, g1, sh2, sc2, g2, sh3, sc3, g3 = jnp.split(mod, N_MOD, axis=-1)

        n = _modulate(_rms(h, ffn1_norm[l]), sh1, sc1)
        h = h + 0.5 * g1[:, None, :] * _swiglu(n, ffn1_w_gate[l], ffn1_w_up[l], ffn1_w_down[l])

        n = _modulate(_rms(h, mix_norm[l]), sh2, sc2)
        proj = n @ w_in[l]
        a = _gmlp_mixer(proj[..., :GMLP_WIDTH], proj[..., GMLP_WIDTH:o1],
                        gmlp_v_norm[l], gmlp_w_s[l], gmlp_b_s[l])
        m = _mla_mixer(proj[..., o1:o2], proj[..., o2:o3], proj[..., o3:], cos, sin,
                       q_lat_norm[l], w_uq[l], kv_lat_norm[l], w_ukv[l],
                       q_nope_norm[l], q_rope_norm[l], k_nope_norm[l], k_rope_norm[l])
        merged = jnp.concatenate([_rms(a, out_norm_gmlp[l]), _rms(m, out_norm_mla[l])], axis=-1)
        h = h + g2[:, None, :] * (merged @ w_out[l])

        n = _modulate(_rms(h, ffn2_norm[l]), sh3, sc3)
        h = h + 0.5 * g3[:, None, :] * _swiglu(n, ffn2_w_gate[l], ffn2_w_up[l], ffn2_w_down[l])

        h = _rms(h, final_norm[l])
    return h
```

```python
import functools

import jax
import jax.numpy as jnp
from jax import lax
from jax.experimental import pallas as pl
from jax.experimental.pallas import tpu as pltpu

D_MODEL = 2048
GMLP_WIDTH = 1024
GMLP_GROUP = 128
GMLP_HEADS = GMLP_WIDTH // GMLP_GROUP
CHUNK = 128
V_HEAD = 128
MLA_WIDTH = 1024
MLA_HEADS = MLA_WIDTH // V_HEAD
QK_NOPE = 128
QK_ROPE = 64
Q_RANK = D_MODEL // 4
KV_RANK = D_MODEL // 8
IN_COLS = 2 * GMLP_WIDTH + Q_RANK + KV_RANK + QK_ROPE
D_FF = 5632
ROPE_THETA = 10000.0
EPS = 1e-6
N_MOD = 9

LANES = 128
QK_PAD = 2 * LANES
IN_COLS_EXT = IN_COLS + QK_ROPE
VMEM_LIMIT = 52 * 1024 * 1024

F32 = jnp.float32
BF16 = jnp.bfloat16


def _dot(a, b):
    return jnp.dot(a, b, preferred_element_type=F32)


def _rms_rows(x, gain):
    return x * lax.rsqrt(jnp.mean(x * x, axis=-1, keepdims=True) + EPS) * gain


def _const_spec(shape):
    zeros = (0,) * len(shape)
    return pl.BlockSpec(shape, lambda *_: zeros, pipeline_mode=pl.Buffered(1))


def _mod_kernel(c_ref, w_ref, b_ref, o_ref):
    c = c_ref[...]
    c_act = (c * jax.nn.sigmoid(c)).astype(BF16)
    o_ref[...] = _dot(c_act, w_ref[...].astype(BF16)) + b_ref[...]


def _mod_call(c_pad, w_ada, b_ada, *, tn=1024):
    rows, d = c_pad.shape
    n = w_ada.shape[1]
    return pl.pallas_call(
        _mod_kernel,
        grid=(n // tn,),
        in_specs=[
            pl.BlockSpec((rows, d), lambda j: (0, 0)),
            pl.BlockSpec((d, tn), lambda j: (0, j)),
            pl.BlockSpec((1, tn), lambda j: (0, j)),
        ],
        out_specs=pl.BlockSpec((rows, tn), lambda j: (0, j)),
        out_shape=jax.ShapeDtypeStruct((rows, n), F32),
        compiler_params=pltpu.CompilerParams(
            dimension_semantics=("arbitrary",), vmem_limit_bytes=VMEM_LIMIT),
        name="adaln_mod",
    )(c_pad, w_ada, b_ada)


def _ffn_kernel(x_ref, sh_ref, sc_ref, g_ref, nw_ref, fin_ref, wg_ref, wu_ref, wd_ref,
                o_ref, n_scr, acc_scr, *, final_norm):
    j = pl.program_id(1)

    @pl.when(j == 0)
    def _():
        n = _rms_rows(x_ref[...], nw_ref[...]) * (1.0 + sc_ref[...]) + sh_ref[...]
        n_scr[...] = n.astype(BF16)
        acc_scr[...] = jnp.zeros_like(acc_scr)

    n = n_scr[...]
    hg = _dot(n, wg_ref[...])
    hu = _dot(n, wu_ref[...])
    hid = (hg * jax.nn.sigmoid(hg) * hu).astype(BF16)
    acc_scr[...] += _dot(hid, wd_ref[...])

    @pl.when(j == pl.num_programs(1) - 1)
    def _():
        h = x_ref[...] + 0.5 * g_ref[...] * acc_scr[...]
        if final_norm:
            h = _rms_rows(h, fin_ref[...])
        o_ref[...] = h


def _ffn_call(x, mod_rows, mod_base, norm_w, fin_w, wg, wu, wd, *, seq, final_norm, tm=512):
    t, d = x.shape
    nf, _, tf = wg.shape
    blocks_per_batch = seq // tm

    def row(k):
        return pl.BlockSpec((None, 1, d), lambda i, j: ((i // blocks_per_batch) * N_MOD + mod_base + k, 0, 0))

    return pl.pallas_call(
        functools.partial(_ffn_kernel, final_norm=final_norm),
        grid=(t // tm, nf),
        in_specs=[
            pl.BlockSpec((tm, d), lambda i, j: (i, 0)),
            row(0), row(1), row(2),
            pl.BlockSpec((1, d), lambda i, j: (0, 0)),
            pl.BlockSpec((1, d), lambda i, j: (0, 0)),
            pl.BlockSpec((None, d, tf), lambda i, j: (j, 0, 0)),
            pl.BlockSpec((None, d, tf), lambda i, j: (j, 0, 0)),
            pl.BlockSpec((None, tf, d), lambda i, j: (j, 0, 0)),
        ],
        out_specs=pl.BlockSpec((tm, d), lambda i, j: (i, 0)),
        out_shape=jax.ShapeDtypeStruct((t, d), F32),
        scratch_shapes=[pltpu.VMEM((tm, d), BF16), pltpu.VMEM((tm, d), F32)],
        compiler_params=pltpu.CompilerParams(
            dimension_semantics=("parallel", "arbitrary"), vmem_limit_bytes=VMEM_LIMIT),
        name="swiglu_ffn",
    )(x, mod_rows, mod_rows, mod_rows, norm_w, fin_w, wg, wu, wd)


def _mix_kernel(h_ref, sh_ref, sc_ref, nw_ref, win_ref, vn_ref, ws_ref, bs_ref,
                qln_ref, wuq_ref, kvn_ref, wukv_ref, qng_ref, qrg_ref, kng_ref, krg_ref,
                ong_ref, pos_ref, freq_ref,
                a_ref, q_ref, k_ref, v_ref,
                u_scr, v_scr, a_scr, *, tm):
    n = (_rms_rows(h_ref[...], nw_ref[...]) * (1.0 + sc_ref[...]) + sh_ref[...]).astype(BF16)

    o1 = GMLP_WIDTH
    o2 = 2 * GMLP_WIDTH
    o3 = o2 + Q_RANK
    o4 = o3 + KV_RANK
    u_scr[...] = jax.nn.gelu(_dot(n, win_ref[:, :o1]))
    vv = jax.nn.gelu(_dot(n, win_ref[:, o1:o2]))
    v_scr[...] = _rms_rows(vv, vn_ref[...]).astype(BF16)
    for c in range(tm // CHUNK):
        rows = slice(c * CHUNK, (c + 1) * CHUNK)
        for g in range(GMLP_HEADS):
            cols = slice(g * GMLP_GROUP, (g + 1) * GMLP_GROUP)
            mixed = _dot(ws_ref[g], v_scr[rows, cols]) + bs_ref[:, cols]
            a_scr[rows, cols] = u_scr[rows, cols] * mixed
    a_ref[...] = _rms_rows(a_scr[...], ong_ref[...]).astype(BF16)

    ang = pos_ref[...].astype(F32) * freq_ref[...]
    lane = lax.broadcasted_iota(jnp.int32, ang.shape, 1)
    sin = jnp.sin(ang)
    rot = jnp.where(lane < QK_ROPE, jnp.cos(ang), jnp.where(lane < QK_ROPE + QK_ROPE // 2, -sin, sin))

    def rope(z, gain):
        w = _rms_rows(z, gain) * rot
        return w + pltpu.roll(w, QK_ROPE, axis=1)

    scale = (QK_NOPE + QK_ROPE) ** -0.5
    q_lat = _rms_rows(_dot(n, win_ref[:, o2:o3]), qln_ref[...]).astype(BF16)
    for hd in range(MLA_HEADS):
        base = hd * QK_PAD
        qq = _dot(q_lat, wuq_ref[:, base:base + QK_PAD])
        q_ref[:, base:base + LANES] = (_rms_rows(qq[:, :LANES], qng_ref[...]) * scale).astype(BF16)
        q_ref[:, base + LANES:base + QK_PAD] = (rope(qq[:, LANES:], qrg_ref[...]) * scale).astype(BF16)

    k_pe = rope(_dot(n, win_ref[:, o4:]), krg_ref[...])
    k_pe = jnp.where(lane < QK_ROPE, k_pe, 0.0).astype(BF16)
    kv_lat = _rms_rows(_dot(n, win_ref[:, o3:o4]), kvn_ref[...]).astype(BF16)
    for hd in range(MLA_HEADS):
        base = hd * QK_PAD
        kv = _dot(kv_lat, wukv_ref[:, base:base + QK_PAD])
        k_ref[:, base:base + LANES] = _rms_rows(kv[:, :LANES], kng_ref[...]).astype(BF16)
        k_ref[:, base + LANES:base + QK_PAD] = k_pe
        v_ref[:, hd * V_HEAD:(hd + 1) * V_HEAD] = kv[:, LANES:].astype(BF16)


def _mix_call(h, mod_rows, mix_norm, win, v_norm, ws, bs_full, q_lat_norm, wuq, kv_lat_norm, wukv,
              qn_g, qr_g, kn_g, kr_g, on_g, pos, freq, *, seq, tm=256):
    t, d = h.shape
    blocks_per_batch = seq // tm

    def row(k):
        return pl.BlockSpec((None, 1, d), lambda i: ((i // blocks_per_batch) * N_MOD + 3 + k, 0, 0))

    def tok(width):
        return pl.BlockSpec((tm, width), lambda i: (i, 0))

    return pl.pallas_call(
        functools.partial(_mix_kernel, tm=tm),
        grid=(t // tm,),
        in_specs=[
            tok(d), row(0), row(1),
            _const_spec(mix_norm.shape), _const_spec(win.shape), _const_spec(v_norm.shape),
            _const_spec(ws.shape), _const_spec(bs_full.shape),
            _const_spec(q_lat_norm.shape), _const_spec(wuq.shape),
            _const_spec(kv_lat_norm.shape), _const_spec(wukv.shape),
            _const_spec(qn_g.shape), _const_spec(qr_g.shape), _const_spec(kn_g.shape),
            _const_spec(kr_g.shape), _const_spec(on_g.shape),
            tok(1), _const_spec(freq.shape),
        ],
        out_specs=[tok(GMLP_WIDTH), tok(MLA_HEADS * QK_PAD), tok(MLA_HEADS * QK_PAD), tok(MLA_WIDTH)],
        out_shape=[
            jax.ShapeDtypeStruct((t, GMLP_WIDTH), BF16),
            jax.ShapeDtypeStruct((t, MLA_HEADS * QK_PAD), BF16),
            jax.ShapeDtypeStruct((t, MLA_HEADS * QK_PAD), BF16),
            jax.ShapeDtypeStruct((t, MLA_WIDTH), BF16),
        ],
        scratch_shapes=[
            pltpu.VMEM((tm, GMLP_WIDTH), F32),
            pltpu.VMEM((tm, GMLP_WIDTH), BF16),
            pltpu.VMEM((tm, GMLP_WIDTH), F32),
        ],
        compiler_params=pltpu.CompilerParams(
            dimension_semantics=("parallel",), vmem_limit_bytes=VMEM_LIMIT),
        name="mixer_in",
    )(h, mod_rows, mod_rows, mix_norm, win, v_norm, ws, bs_full, q_lat_norm, wuq, kv_lat_norm, wukv,
      qn_g, qr_g, kn_g, kr_g, on_g, pos, freq)


def _attn_kernel(q_ref, k_ref, v_ref, o_ref):
    s = lax.dot_general(q_ref[...], k_ref[...], (((1,), (1,)), ((), ())), preferred_element_type=F32)
    p = jnp.exp(s - jnp.max(s, axis=-1, keepdims=True))
    denom = jnp.sum(p, axis=-1, keepdims=True)
    o_ref[...] = _dot(p.astype(BF16), v_ref[...]) / denom


def _attn_call(q, k, v, *, batch, seq, tq=256):
    t = q.shape[0]
    nq = seq // tq
    return pl.pallas_call(
        _attn_kernel,
        grid=(batch, MLA_HEADS, nq),
        in_specs=[
            pl.BlockSpec((tq, QK_PAD), lambda b, h, i: (b * nq + i, h)),
            pl.BlockSpec((seq, QK_PAD), lambda b, h, i: (b, h)),
            pl.BlockSpec((seq, V_HEAD), lambda b, h, i: (b, h)),
        ],
        out_specs=pl.BlockSpec((tq, V_HEAD), lambda b, h, i: (b * nq + i, h)),
        out_shape=jax.ShapeDtypeStruct((t, MLA_WIDTH), F32),
        compiler_params=pltpu.CompilerParams(
            dimension_semantics=("parallel", "parallel", "arbitrary"), vmem_limit_bytes=VMEM_LIMIT),
        name="mla_attention",
    )(q, k, v)


def _outproj_kernel(h_ref, a_ref, m_ref, g_ref, mg_ref, wo_ref, o_ref):
    m_n = _rms_rows(m_ref[...], mg_ref[...]).astype(BF16)
    y = _dot(a_ref[...], wo_ref[:GMLP_WIDTH, :]) + _dot(m_n, wo_ref[GMLP_WIDTH:, :])
    o_ref[...] = h_ref[...] + g_ref[...] * y


def _outproj_call(h, a, m, mod_rows, mla_gain, wo, *, seq, tm=512):
    t, d = h.shape
    blocks_per_batch = seq // tm
    return pl.pallas_call(
        _outproj_kernel,
        grid=(t // tm,),
        in_specs=[
            pl.BlockSpec((tm, d), lambda i: (i, 0)),
            pl.BlockSpec((tm, GMLP_WIDTH), lambda i: (i, 0)),
            pl.BlockSpec((tm, MLA_WIDTH), lambda i: (i, 0)),
            pl.BlockSpec((None, 1, d), lambda i: ((i // blocks_per_batch) * N_MOD + 5, 0, 0)),
            _const_spec(mla_gain.shape), _const_spec(wo.shape),
        ],
        out_specs=pl.BlockSpec((tm, d), lambda i: (i, 0)),
        out_shape=jax.ShapeDtypeStruct((t, d), F32),
        compiler_params=pltpu.CompilerParams(
            dimension_semantics=("parallel",), vmem_limit_bytes=VMEM_LIMIT),
        name="mixer_out",
    )(h, a, m, mod_rows, mla_gain, wo)


def _ffn_weights(w_gate, w_up, w_down, tf=512):
    d, f = w_gate.shape
    nf = f // tf
    wg = w_gate.astype(BF16).reshape(d, nf, tf).transpose(1, 0, 2)
    wu = w_up.astype(BF16).reshape(d, nf, tf).transpose(1, 0, 2)
    wd = w_down.astype(BF16).reshape(nf, tf, d)
    return wg, wu, wd


def _half_swap(x):
    half = x.shape[-1] // 2
    return jnp.concatenate([x[..., half:], x[..., :half]], axis=-1)


def kernel(x, c, positions, w_ada, b_ada, ffn1_norm, ffn1_w_gate, ffn1_w_up, ffn1_w_down, mix_norm, w_in, gmlp_v_norm, gmlp_w_s, gmlp_b_s, q_lat_norm, w_uq, kv_lat_norm, w_ukv, q_nope_norm, q_rope_norm, k_nope_norm, k_rope_norm, out_norm_gmlp, out_norm_mla, w_out, ffn2_norm, ffn2_w_gate, ffn2_w_up, ffn2_w_down, final_norm):
    batch, seq, d = x.shape
    depth = w_ada.shape[0]
    t = batch * seq

    inv_freq = ROPE_THETA ** (-jnp.arange(0, QK_ROPE, 2, dtype=F32) / QK_ROPE)
    freq = jnp.tile(inv_freq, 4)[None, :]
    pos = positions.reshape(t, 1)
    c_pad = jnp.pad(c, ((0, 8 - batch), (0, 0)))

    def row(v):
        return v[None, :]

    h = x.reshape(t, d)
    for l in range(depth):
        mod = _mod_call(c_pad, w_ada[l], row(b_ada[l]))
        mod_rows = mod[:batch].reshape(batch * N_MOD, 1, d)

        h = _ffn_call(h, mod_rows, 0, row(ffn1_norm[l]), row(final_norm[l]),
                      *_ffn_weights(ffn1_w_gate[l], ffn1_w_up[l], ffn1_w_down[l]),
                      seq=seq, final_norm=False)

        win = jnp.concatenate([w_in[l], _half_swap(w_in[l][:, IN_COLS - QK_ROPE:])], axis=1).astype(BF16)
        uq = w_uq[l].reshape(Q_RANK, MLA_HEADS, QK_NOPE + QK_ROPE)
        uq_rope = uq[..., QK_NOPE:]
        wuq = jnp.concatenate([uq[..., :QK_NOPE], uq_rope, _half_swap(uq_rope)], axis=-1)
        wuq = wuq.reshape(Q_RANK, MLA_HEADS * QK_PAD).astype(BF16)
        bs_full = jnp.repeat(gmlp_b_s[l].T, GMLP_GROUP, axis=1)
        a, q, k, v = _mix_call(
            h, mod_rows, row(mix_norm[l]), win, row(gmlp_v_norm[l]), gmlp_w_s[l].astype(BF16), bs_full,
            row(q_lat_norm[l]), wuq, row(kv_lat_norm[l]), w_ukv[l].astype(BF16),
            row(q_nope_norm[l]), row(jnp.concatenate([q_rope_norm[l], _half_swap(q_rope_norm[l])])),
            row(k_nope_norm[l]), row(jnp.concatenate([k_rope_norm[l], _half_swap(k_rope_norm[l])])),
            row(out_norm_gmlp[l]), pos, freq, seq=seq)

        m = _attn_call(q, k, v, batch=batch, seq=seq)
        h = _outproj_call(h, a, m, mod_rows, row(out_norm_mla[l]), w_out[l].astype(BF16), seq=seq)

        h = _ffn_call(h, mod_rows, 6, row(ffn2_norm[l]), row(final_norm[l]),
                      *_ffn_weights(ffn2_w_gate[l], ffn2_w_up[l], ffn2_w_down[l]),
                      seq=seq, final_norm=True)
    return h.reshape(batch, seq, d)
```

```python
import functools

import jax
import jax.numpy as jnp
from jax import lax
from jax.experimental import pallas as pl
from jax.experimental.pallas import tpu as pltpu

D_MODEL = 2048
GMLP_WIDTH = 1024
GMLP_GROUP = 128
GMLP_HEADS = GMLP_WIDTH // GMLP_GROUP
CHUNK = 128
V_HEAD = 128
MLA_WIDTH = 1024
MLA_HEADS = MLA_WIDTH // V_HEAD
QK_NOPE = 128
QK_ROPE = 64
Q_RANK = D_MODEL // 4
KV_RANK = D_MODEL // 8
IN_COLS = 2 * GMLP_WIDTH + Q_RANK + KV_RANK + QK_ROPE
D_FF = 5632
ROPE_THETA = 10000.0
EPS = 1e-6
N_MOD = 9
LOG2_E = 1.4426950408889634

LANES = 128
QK_PAD = 2 * LANES
IN_COLS_EXT = IN_COLS + QK_ROPE
VMEM_LIMIT = 52 * 1024 * 1024

F32 = jnp.float32
BF16 = jnp.bfloat16


def _dot(a, b):
    return jnp.dot(a, b, preferred_element_type=F32)


def _rms_rows(x, gain):
    return x * lax.rsqrt(jnp.mean(x * x, axis=-1, keepdims=True) + EPS) * gain


def _const_spec(shape):
    zeros = (0,) * len(shape)
    return pl.BlockSpec(shape, lambda *_: zeros, pipeline_mode=pl.Buffered(1))


def _mod_kernel(c_ref, w_ref, b_ref, o_ref):
    c = c_ref[...]
    c_act = (c * jax.nn.sigmoid(c)).astype(BF16)
    o_ref[...] = _dot(c_act, w_ref[...].astype(BF16)) + b_ref[...]


def _mod_call(c_pad, w_ada, b_ada, *, tn=1024):
    rows, d = c_pad.shape
    n = w_ada.shape[1]
    return pl.pallas_call(
        _mod_kernel,
        grid=(n // tn,),
        in_specs=[
            pl.BlockSpec((rows, d), lambda j: (0, 0)),
            pl.BlockSpec((d, tn), lambda j: (0, j)),
            pl.BlockSpec((1, tn), lambda j: (0, j)),
        ],
        out_specs=pl.BlockSpec((rows, tn), lambda j: (0, j)),
        out_shape=jax.ShapeDtypeStruct((rows, n), F32),
        compiler_params=pltpu.CompilerParams(
            dimension_semantics=("arbitrary",), vmem_limit_bytes=VMEM_LIMIT),
        name="adaln_mod",
    )(c_pad, w_ada, b_ada)


def _ffn_kernel(x_ref, sh_ref, sc_ref, g_ref, nw_ref, fin_ref, wg_ref, wu_ref, wd_ref,
                o_ref, n_scr, acc_scr, *, final_norm):
    j = pl.program_id(1)

    @pl.when(j == 0)
    def _():
        gain = nw_ref[...] * (1.0 + sc_ref[...])
        n_scr[...] = (_rms_rows(x_ref[...], gain) + sh_ref[...]).astype(BF16)
        acc_scr[...] = jnp.zeros_like(acc_scr)

    n = n_scr[...]
    hg = _dot(n, wg_ref[...])
    hu = _dot(n, wu_ref[...])
    hid = (hg * jax.nn.sigmoid(hg) * hu).astype(BF16)
    acc_scr[...] += _dot(hid, wd_ref[...])

    @pl.when(j == pl.num_programs(1) - 1)
    def _():
        h = x_ref[...] + (0.5 * g_ref[...]) * acc_scr[...]
        if final_norm:
            h = _rms_rows(h, fin_ref[...])
        o_ref[...] = h


def _ffn_call(x, mod_rows, mod_base, norm_w, fin_w, wg, wu, wd, *, seq, final_norm, tm=512, tf=512):
    t, d = x.shape
    nf = wg.shape[1] // tf
    blocks_per_batch = seq // tm

    def row(k):
        return pl.BlockSpec((None, 1, d), lambda i, j: ((i // blocks_per_batch) * N_MOD + mod_base + k, 0, 0))

    return pl.pallas_call(
        functools.partial(_ffn_kernel, final_norm=final_norm),
        grid=(t // tm, nf),
        in_specs=[
            pl.BlockSpec((tm, d), lambda i, j: (i, 0)),
            row(0), row(1), row(2),
            pl.BlockSpec((1, d), lambda i, j: (0, 0)),
            pl.BlockSpec((1, d), lambda i, j: (0, 0)),
            pl.BlockSpec((d, tf), lambda i, j: (0, j)),
            pl.BlockSpec((d, tf), lambda i, j: (0, j)),
            pl.BlockSpec((tf, d), lambda i, j: (j, 0)),
        ],
        out_specs=pl.BlockSpec((tm, d), lambda i, j: (i, 0)),
        out_shape=jax.ShapeDtypeStruct((t, d), F32),
        scratch_shapes=[pltpu.VMEM((tm, d), BF16), pltpu.VMEM((tm, d), F32)],
        compiler_params=pltpu.CompilerParams(
            dimension_semantics=("parallel", "arbitrary"), vmem_limit_bytes=VMEM_LIMIT),
        name="swiglu_ffn",
    )(x, mod_rows, mod_rows, mod_rows, norm_w, fin_w, wg, wu, wd)


def _mix_kernel(h_ref, sh_ref, sc_ref, nw_ref, win_ref, vn_ref, ws_ref, bs_ref,
                qln_ref, wuq_ref, kvn_ref, wukv_ref, qng_ref, qrg_ref, kng_ref, krg_ref,
                ong_ref, pos_ref, freq_ref,
                a_ref, q_ref, k_ref, v_ref,
                u_scr, v_scr, a_scr, *, tm):
    n = (_rms_rows(h_ref[...], nw_ref[...] * (1.0 + sc_ref[...])) + sh_ref[...]).astype(BF16)

    o1 = GMLP_WIDTH
    o2 = 2 * GMLP_WIDTH
    o3 = o2 + Q_RANK
    o4 = o3 + KV_RANK
    u_scr[...] = jax.nn.gelu(_dot(n, win_ref[:, :o1]))
    vv = jax.nn.gelu(_dot(n, win_ref[:, o1:o2]))
    v_scr[...] = _rms_rows(vv, vn_ref[...]).astype(BF16)
    for c in range(tm // CHUNK):
        rows = slice(c * CHUNK, (c + 1) * CHUNK)
        for g in range(GMLP_HEADS):
            cols = slice(g * GMLP_GROUP, (g + 1) * GMLP_GROUP)
            mixed = _dot(ws_ref[g], v_scr[rows, cols]) + bs_ref[:, cols]
            a_scr[rows, cols] = u_scr[rows, cols] * mixed
    a_ref[...] = _rms_rows(a_scr[...], ong_ref[...]).astype(BF16)

    ang = pos_ref[...].astype(F32) * freq_ref[...]
    lane = lax.broadcasted_iota(jnp.int32, ang.shape, 1)
    sin = jnp.sin(ang)
    rot = jnp.where(lane < QK_ROPE, jnp.cos(ang), jnp.where(lane < QK_ROPE + QK_ROPE // 2, -sin, sin))

    def rope(z, gain):
        w = _rms_rows(z, gain) * rot
        return w + pltpu.roll(w, QK_ROPE, axis=1)

    scale = (QK_NOPE + QK_ROPE) ** -0.5 * LOG2_E
    q_lat = _rms_rows(_dot(n, win_ref[:, o2:o3]), qln_ref[...]).astype(BF16)
    for hd in range(MLA_HEADS):
        base = hd * QK_PAD
        qq = _dot(q_lat, wuq_ref[:, base:base + QK_PAD])
        q_ref[:, base:base + LANES] = (_rms_rows(qq[:, :LANES], qng_ref[...]) * scale).astype(BF16)
        q_ref[:, base + LANES:base + QK_PAD] = (rope(qq[:, LANES:], qrg_ref[...]) * scale).astype(BF16)

    k_pe = rope(_dot(n, win_ref[:, o4:]), krg_ref[...])
    k_pe = jnp.where(lane < QK_ROPE, k_pe, 0.0).astype(BF16)
    kv_lat = _rms_rows(_dot(n, win_ref[:, o3:o4]), kvn_ref[...]).astype(BF16)
    for hd in range(MLA_HEADS):
        base = hd * QK_PAD
        kv = _dot(kv_lat, wukv_ref[:, base:base + QK_PAD])
        k_ref[:, base:base + LANES] = _rms_rows(kv[:, :LANES], kng_ref[...]).astype(BF16)
        k_ref[:, base + LANES:base + QK_PAD] = k_pe
        v_ref[:, hd * V_HEAD:(hd + 1) * V_HEAD] = kv[:, LANES:].astype(BF16)


def _mix_call(h, mod_rows, mix_norm, win, v_norm, ws, bs_full, q_lat_norm, wuq, kv_lat_norm, wukv,
              qn_g, qr_g, kn_g, kr_g, on_g, pos, freq, *, seq, tm=512):
    t, d = h.shape
    blocks_per_batch = seq // tm

    def row(k):
        return pl.BlockSpec((None, 1, d), lambda i: ((i // blocks_per_batch) * N_MOD + 3 + k, 0, 0))

    def tok(width):
        return pl.BlockSpec((tm, width), lambda i: (i, 0))

    return pl.pallas_call(
        functools.partial(_mix_kernel, tm=tm),
        grid=(t // tm,),
        in_specs=[
            tok(d), row(0), row(1),
            _const_spec(mix_norm.shape), _const_spec(win.shape), _const_spec(v_norm.shape),
            _const_spec(ws.shape), _const_spec(bs_full.shape),
            _const_spec(q_lat_norm.shape), _const_spec(wuq.shape),
            _const_spec(kv_lat_norm.shape), _const_spec(wukv.shape),
            _const_spec(qn_g.shape), _const_spec(qr_g.shape), _const_spec(kn_g.shape),
            _const_spec(kr_g.shape), _const_spec(on_g.shape),
            tok(1), _const_spec(freq.shape),
        ],
        out_specs=[tok(GMLP_WIDTH), tok(MLA_HEADS * QK_PAD), tok(MLA_HEADS * QK_PAD), tok(MLA_WIDTH)],
        out_shape=[
            jax.ShapeDtypeStruct((t, GMLP_WIDTH), BF16),
            jax.ShapeDtypeStruct((t, MLA_HEADS * QK_PAD), BF16),
            jax.ShapeDtypeStruct((t, MLA_HEADS * QK_PAD), BF16),
            jax.ShapeDtypeStruct((t, MLA_WIDTH), BF16),
        ],
        scratch_shapes=[
            pltpu.VMEM((tm, GMLP_WIDTH), F32),
            pltpu.VMEM((tm, GMLP_WIDTH), BF16),
            pltpu.VMEM((tm, GMLP_WIDTH), F32),
        ],
        compiler_params=pltpu.CompilerParams(
            dimension_semantics=("parallel",), vmem_limit_bytes=VMEM_LIMIT),
        name="mixer_in",
    )(h, mod_rows, mod_rows, mix_norm, win, v_norm, ws, bs_full, q_lat_norm, wuq, kv_lat_norm, wukv,
      qn_g, qr_g, kn_g, kr_g, on_g, pos, freq)


def _attn_kernel(q_ref, k_ref, v_ref, o_ref, vx_scr, s_scr, p_scr, *, sub, nsub):
    vx_scr[:, :V_HEAD] = v_ref[...]
    lane = lax.broadcasted_iota(jnp.int32, (vx_scr.shape[0], LANES), 1)
    vx_scr[:, V_HEAD:] = jnp.where(lane == 0, 1.0, 0.0).astype(BF16)

    def rows(i):
        return pl.ds(pl.multiple_of(i * sub, sub), sub)

    def qk(i, slot):
        s_scr[slot] = lax.dot_general(q_ref[rows(i), :], k_ref[...], (((1,), (1,)), ((), ())),
                                      preferred_element_type=F32)

    def softmax(slot):
        s = s_scr[slot]
        p_scr[slot] = jnp.exp2(s - jnp.max(s, axis=-1, keepdims=True)).astype(BF16)

    def pv(i, slot):
        ox = _dot(p_scr[slot], vx_scr[...])
        o_ref[rows(i), :] = ox[:, :V_HEAD] / ox[:, V_HEAD:V_HEAD + 1]

    qk(0, 0)
    softmax(0)
    qk(1, 1)

    def stage_pair(j, carry):
        i = 2 * j
        qk(i + 2, 0)
        softmax(1)
        pv(i, 0)
        qk(i + 3, 1)
        softmax(0)
        pv(i + 1, 1)
        return carry

    lax.fori_loop(0, nsub // 2 - 1, stage_pair, 0)
    softmax(1)
    pv(nsub - 2, 0)
    pv(nsub - 1, 1)


def _attn_call(q, k, v, *, batch, seq, sub=256):
    t = q.shape[0]
    nsub = seq // sub
    assert nsub % 2 == 0 and nsub >= 4
    return pl.pallas_call(
        functools.partial(_attn_kernel, sub=sub, nsub=nsub),
        grid=(batch, MLA_HEADS),
        in_specs=[
            pl.BlockSpec((seq, QK_PAD), lambda b, h: (b, h)),
            pl.BlockSpec((seq, QK_PAD), lambda b, h: (b, h)),
            pl.BlockSpec((seq, V_HEAD), lambda b, h: (b, h)),
        ],
        out_specs=pl.BlockSpec((seq, V_HEAD), lambda b, h: (b, h)),
        out_shape=jax.ShapeDtypeStruct((t, MLA_WIDTH), F32),
        scratch_shapes=[
            pltpu.VMEM((seq, 2 * V_HEAD), BF16),
            pltpu.VMEM((2, sub, seq), F32),
            pltpu.VMEM((2, sub, seq), BF16),
        ],
        compiler_params=pltpu.CompilerParams(
            dimension_semantics=("parallel", "parallel"), vmem_limit_bytes=VMEM_LIMIT),
        name="mla_attention",
    )(q, k, v)


def _outproj_kernel(h_ref, a_ref, m_ref, g_ref, mg_ref, wo_ref, o_ref):
    m_n = _rms_rows(m_ref[...], mg_ref[...]).astype(BF16)
    y = _dot(a_ref[...], wo_ref[:GMLP_WIDTH, :]) + _dot(m_n, wo_ref[GMLP_WIDTH:, :])
    o_ref[...] = h_ref[...] + g_ref[...] * y


def _outproj_call(h, a, m, mod_rows, mla_gain, wo, *, seq, tm=512):
    t, d = h.shape
    blocks_per_batch = seq // tm
    return pl.pallas_call(
        _outproj_kernel,
        grid=(t // tm,),
        in_specs=[
            pl.BlockSpec((tm, d), lambda i: (i, 0)),
            pl.BlockSpec((tm, GMLP_WIDTH), lambda i: (i, 0)),
            pl.BlockSpec((tm, MLA_WIDTH), lambda i: (i, 0)),
            pl.BlockSpec((None, 1, d), lambda i: ((i // blocks_per_batch) * N_MOD + 5, 0, 0)),
            _const_spec(mla_gain.shape), _const_spec(wo.shape),
        ],
        out_specs=pl.BlockSpec((tm, d), lambda i: (i, 0)),
        out_shape=jax.ShapeDtypeStruct((t, d), F32),
        compiler_params=pltpu.CompilerParams(
            dimension_semantics=("parallel",), vmem_limit_bytes=VMEM_LIMIT),
        name="mixer_out",
    )(h, a, m, mod_rows, mla_gain, wo)


def _ffn_weights(w_gate, w_up, w_down):
    return w_gate.astype(BF16), w_up.astype(BF16), w_down.astype(BF16)


def _half_swap(x):
    half = x.shape[-1] // 2
    return jnp.concatenate([x[..., half:], x[..., :half]], axis=-1)


def kernel(x, c, positions, w_ada, b_ada, ffn1_norm, ffn1_w_gate, ffn1_w_up, ffn1_w_down, mix_norm, w_in, gmlp_v_norm, gmlp_w_s, gmlp_b_s, q_lat_norm, w_uq, kv_lat_norm, w_ukv, q_nope_norm, q_rope_norm, k_nope_norm, k_rope_norm, out_norm_gmlp, out_norm_mla, w_out, ffn2_norm, ffn2_w_gate, ffn2_w_up, ffn2_w_down, final_norm):
    batch, seq, d = x.shape
    depth = w_ada.shape[0]
    t = batch * seq

    inv_freq = ROPE_THETA ** (-jnp.arange(0, QK_ROPE, 2, dtype=F32) / QK_ROPE)
    freq = jnp.tile(inv_freq, 4)[None, :]
    pos = positions.reshape(t, 1)
    c_pad = jnp.pad(c, ((0, 8 - batch), (0, 0)))

    def row(v):
        return v[None, :]

    h = x.reshape(t, d)
    for l in range(depth):
        mod = _mod_call(c_pad, w_ada[l], row(b_ada[l]))
        mod_rows = mod[:batch].reshape(batch * N_MOD, 1, d)

        h = _ffn_call(h, mod_rows, 0, row(ffn1_norm[l]), row(final_norm[l]),
                      *_ffn_weights(ffn1_w_gate[l], ffn1_w_up[l], ffn1_w_down[l]),
                      seq=seq, final_norm=False)

        win = jnp.concatenate([w_in[l], _half_swap(w_in[l][:, IN_COLS - QK_ROPE:])], axis=1).astype(BF16)
        uq = w_uq[l].reshape(Q_RANK, MLA_HEADS, QK_NOPE + QK_ROPE)
        uq_rope = uq[..., QK_NOPE:]
        wuq = jnp.concatenate([uq[..., :QK_NOPE], uq_rope, _half_swap(uq_rope)], axis=-1)
        wuq = wuq.reshape(Q_RANK, MLA_HEADS * QK_PAD).astype(BF16)
        bs_full = jnp.repeat(gmlp_b_s[l].T, GMLP_GROUP, axis=1)
        a, q, k, v = _mix_call(
            h, mod_rows, row(mix_norm[l]), win, row(gmlp_v_norm[l]), gmlp_w_s[l].astype(BF16), bs_full,
            row(q_lat_norm[l]), wuq, row(kv_lat_norm[l]), w_ukv[l].astype(BF16),
            row(q_nope_norm[l]), row(jnp.concatenate([q_rope_norm[l], _half_swap(q_rope_norm[l])])),
            row(k_nope_norm[l]), row(jnp.concatenate([k_rope_norm[l], _half_swap(k_rope_norm[l])])),
            row(out_norm_gmlp[l]), pos, freq, seq=seq)

        m = _attn_call(q, k, v, batch=batch, seq=seq)
        h = _outproj_call(h, a, m, mod_rows, row(out_norm_mla[l]), w_out[l].astype(BF16), seq=seq)

        h = _ffn_call(h, mod_rows, 6, row(ffn2_norm[l]), row(final_norm[l]),
                      *_ffn_weights(ffn2_w_gate[l], ffn2_w_up[l], ffn2_w_down[l]),
                      seq=seq, final_norm=True)
    return h.reshape(batch, seq, d)
```

```python
import functools

import jax
import jax.numpy as jnp
from jax import lax
from jax.experimental import pallas as pl
from jax.experimental.pallas import tpu as pltpu

D_MODEL = 2048
GMLP_WIDTH = 1024
GMLP_GROUP = 128
GMLP_HEADS = GMLP_WIDTH // GMLP_GROUP
CHUNK = 128
V_HEAD = 128
MLA_WIDTH = 1024
MLA_HEADS = MLA_WIDTH // V_HEAD
QK_NOPE = 128
QK_ROPE = 64
Q_RANK = D_MODEL // 4
KV_RANK = D_MODEL // 8
IN_COLS = 2 * GMLP_WIDTH + Q_RANK + KV_RANK + QK_ROPE
D_FF = 5632
ROPE_THETA = 10000.0
EPS = 1e-6
N_MOD = 9
LOG2_E = 1.4426950408889634

LANES = 128
QK_PAD = 2 * LANES
IN_COLS_EXT = IN_COLS + QK_ROPE
VMEM_LIMIT = 56 * 1024 * 1024
ATTN_UNROLL = 4
FFN_TM = 512
FFN_TF = 512

F32 = jnp.float32
BF16 = jnp.bfloat16


def _dot(a, b):
    return jnp.dot(a, b, preferred_element_type=F32)


def _rms_rows(x, gain):
    return x * lax.rsqrt(jnp.mean(x * x, axis=-1, keepdims=True) + EPS) * gain


def _const_spec(shape):
    zeros = (0,) * len(shape)
    return pl.BlockSpec(shape, lambda *_: zeros, pipeline_mode=pl.Buffered(1))


def _mod_kernel(c_ref, w_ref, b_ref, o_ref):
    c = c_ref[...]
    c_act = (c * jax.nn.sigmoid(c)).astype(BF16)
    o_ref[...] = _dot(c_act, w_ref[...].astype(BF16)) + b_ref[...]


def _mod_call(c_pad, w_ada, b_ada, *, tn=1024):
    rows, d = c_pad.shape
    n = w_ada.shape[1]
    return pl.pallas_call(
        _mod_kernel,
        grid=(n // tn,),
        in_specs=[
            pl.BlockSpec((rows, d), lambda j: (0, 0)),
            pl.BlockSpec((d, tn), lambda j: (0, j)),
            pl.BlockSpec((1, tn), lambda j: (0, j)),
        ],
        out_specs=pl.BlockSpec((rows, tn), lambda j: (0, j)),
        out_shape=jax.ShapeDtypeStruct((rows, n), F32),
        compiler_params=pltpu.CompilerParams(
            dimension_semantics=("arbitrary",), vmem_limit_bytes=VMEM_LIMIT),
        name="adaln_mod",
    )(c_pad, w_ada, b_ada)


def _ffn_kernel(x_ref, sh_ref, sc_ref, g_ref, nw_ref, fin_ref, wg_ref, wu_ref, wd_ref, *rest,
                final_norm, cast_every_step):
    n_cast = len(cast_every_step)
    cast_in = rest[:n_cast]
    o_ref, *cast_out = rest[n_cast:2 * n_cast + 1]
    n_scr, acc_scr = rest[2 * n_cast + 1:]
    j = pl.program_id(1)

    def cast(every_step):
        for src, dst, flag in zip(cast_in, cast_out, cast_every_step):
            if flag == every_step:
                dst[...] = src[...].astype(BF16)

    cast(True)

    @pl.when(j == 0)
    def _():
        cast(False)
        gain = nw_ref[...] * (1.0 + sc_ref[...])
        n_scr[...] = (_rms_rows(x_ref[...], gain) + sh_ref[...]).astype(BF16)
        acc_scr[...] = jnp.zeros_like(acc_scr)

    n = n_scr[...]
    hg = _dot(n, wg_ref[...])
    hu = _dot(n, wu_ref[...])
    hid = (hg * jax.nn.sigmoid(hg) * hu).astype(BF16)
    acc_scr[...] += _dot(hid, wd_ref[...])

    @pl.when(j == pl.num_programs(1) - 1)
    def _():
        h = x_ref[...] + (0.5 * g_ref[...]) * acc_scr[...]
        if final_norm:
            h = _rms_rows(h, fin_ref[...])
        o_ref[...] = h


def _ffn_call(x, mod_rows, mod_base, norm_w, fin_w, wg, wu, wd, *, seq, final_norm, cast_jobs=(),
              tm=512, tf=512):
    t, d = x.shape
    nf = wg.shape[1] // tf
    blocks_per_batch = seq // tm

    def row(k):
        return pl.BlockSpec((None, 1, d), lambda i, j: ((i // blocks_per_batch) * N_MOD + mod_base + k, 0, 0))

    cast_arrays = [job[0] for job in cast_jobs]
    cast_specs = [pl.BlockSpec(job[1], job[2]) for job in cast_jobs]
    outs = pl.pallas_call(
        functools.partial(_ffn_kernel, final_norm=final_norm,
                          cast_every_step=tuple(job[3] for job in cast_jobs)),
        grid=(t // tm, nf),
        in_specs=[
            pl.BlockSpec((tm, d), lambda i, j: (i, 0)),
            row(0), row(1), row(2),
            pl.BlockSpec((1, d), lambda i, j: (0, 0)),
            pl.BlockSpec((1, d), lambda i, j: (0, 0)),
            pl.BlockSpec((d, tf), lambda i, j: (0, j)),
            pl.BlockSpec((d, tf), lambda i, j: (0, j)),
            pl.BlockSpec((tf, d), lambda i, j: (j, 0)),
            *cast_specs,
        ],
        out_specs=[pl.BlockSpec((tm, d), lambda i, j: (i, 0)), *cast_specs],
        out_shape=[jax.ShapeDtypeStruct((t, d), F32),
                   *(jax.ShapeDtypeStruct(a.shape, BF16) for a in cast_arrays)],
        scratch_shapes=[pltpu.VMEM((tm, d), BF16), pltpu.VMEM((tm, d), F32)],
        compiler_params=pltpu.CompilerParams(
            dimension_semantics=("parallel", "arbitrary"), vmem_limit_bytes=VMEM_LIMIT),
        name="swiglu_ffn",
    )(x, mod_rows, mod_rows, mod_rows, norm_w, fin_w, wg, wu, wd, *cast_arrays)
    return outs[0], outs[1:]


def _mix_kernel(h_ref, sh_ref, sc_ref, nw_ref, win_ref, wkpe_ref, vn_ref, ws_ref, bs_ref,
                qln_ref, wuq_ref, kvn_ref, wukv_ref, qng_ref, qrg_ref, kng_ref, krg_ref,
                ong_ref, pos_ref, freq_ref,
                a_ref, q_ref, k_ref, v_ref,
                u_scr, v_scr, a_scr, *, tm):
    n = (_rms_rows(h_ref[...], nw_ref[...] * (1.0 + sc_ref[...])) + sh_ref[...]).astype(BF16)

    o1 = GMLP_WIDTH
    o2 = 2 * GMLP_WIDTH
    o3 = o2 + Q_RANK
    o4 = o3 + KV_RANK
    u_scr[...] = jax.nn.gelu(_dot(n, win_ref[:, :o1]))
    vv = jax.nn.gelu(_dot(n, win_ref[:, o1:o2]))
    v_scr[...] = _rms_rows(vv, vn_ref[...]).astype(BF16)
    for c in range(tm // CHUNK):
        rows = slice(c * CHUNK, (c + 1) * CHUNK)
        for g in range(GMLP_HEADS):
            cols = slice(g * GMLP_GROUP, (g + 1) * GMLP_GROUP)
            mixed = _dot(ws_ref[g], v_scr[rows, cols]) + bs_ref[:, cols]
            a_scr[rows, cols] = u_scr[rows, cols] * mixed
    a_ref[...] = _rms_rows(a_scr[...], ong_ref[...]).astype(BF16)

    ang = pos_ref[...].astype(F32) * freq_ref[...]
    lane = lax.broadcasted_iota(jnp.int32, ang.shape, 1)
    sin = jnp.sin(ang)
    rot = jnp.where(lane < QK_ROPE, jnp.cos(ang), jnp.where(lane < QK_ROPE + QK_ROPE // 2, -sin, sin))

    def rope(z, gain):
        w = _rms_rows(z, gain) * rot
        return w + pltpu.roll(w, QK_ROPE, axis=1)

    scale = (QK_NOPE + QK_ROPE) ** -0.5 * LOG2_E
    q_lat = _rms_rows(_dot(n, win_ref[:, o2:o3]), qln_ref[...]).astype(BF16)
    for hd in range(MLA_HEADS):
        base = hd * QK_PAD
        qq = _dot(q_lat, wuq_ref[:, base:base + QK_PAD])
        q_ref[:, base:base + LANES] = (_rms_rows(qq[:, :LANES], qng_ref[...]) * scale).astype(BF16)
        q_ref[:, base + LANES:base + QK_PAD] = (rope(qq[:, LANES:], qrg_ref[...]) * scale).astype(BF16)

    k_pe = rope(_dot(n, wkpe_ref[...]), krg_ref[...])
    k_pe = jnp.where(lane < QK_ROPE, k_pe, 0.0).astype(BF16)
    kv_lat = _rms_rows(_dot(n, win_ref[:, o3:o4]), kvn_ref[...]).astype(BF16)
    for hd in range(MLA_HEADS):
        base = hd * QK_PAD
        kv = _dot(kv_lat, wukv_ref[:, base:base + QK_PAD])
        k_ref[:, base:base + LANES] = _rms_rows(kv[:, :LANES], kng_ref[...]).astype(BF16)
        k_ref[:, base + LANES:base + QK_PAD] = k_pe
        v_ref[:, hd * V_HEAD:(hd + 1) * V_HEAD] = kv[:, LANES:].astype(BF16)


def _mix_call(h, mod_rows, mix_norm, win, wkpe, v_norm, ws, bs_full, q_lat_norm, wuq, kv_lat_norm, wukv,
              qn_g, qr_g, kn_g, kr_g, on_g, pos, freq, *, seq, tm=512):
    t, d = h.shape
    blocks_per_batch = seq // tm

    def row(k):
        return pl.BlockSpec((None, 1, d), lambda i: ((i // blocks_per_batch) * N_MOD + 3 + k, 0, 0))

    def tok(width):
        return pl.BlockSpec((tm, width), lambda i: (i, 0))

    return pl.pallas_call(
        functools.partial(_mix_kernel, tm=tm),
        grid=(t // tm,),
        in_specs=[
            tok(d), row(0), row(1),
            _const_spec(mix_norm.shape), _const_spec(win.shape), _const_spec(wkpe.shape),
            _const_spec(v_norm.shape),
            _const_spec(ws.shape), _const_spec(bs_full.shape),
            _const_spec(q_lat_norm.shape), _const_spec(wuq.shape),
            _const_spec(kv_lat_norm.shape), _const_spec(wukv.shape),
            _const_spec(qn_g.shape), _const_spec(qr_g.shape), _const_spec(kn_g.shape),
            _const_spec(kr_g.shape), _const_spec(on_g.shape),
            tok(1), _const_spec(freq.shape),
        ],
        out_specs=[tok(GMLP_WIDTH), tok(MLA_HEADS * QK_PAD), tok(MLA_HEADS * QK_PAD), tok(MLA_WIDTH)],
        out_shape=[
            jax.ShapeDtypeStruct((t, GMLP_WIDTH), BF16),
            jax.ShapeDtypeStruct((t, MLA_HEADS * QK_PAD), BF16),
            jax.ShapeDtypeStruct((t, MLA_HEADS * QK_PAD), BF16),
            jax.ShapeDtypeStruct((t, MLA_WIDTH), BF16),
        ],
        scratch_shapes=[
            pltpu.VMEM((tm, GMLP_WIDTH), F32),
            pltpu.VMEM((tm, GMLP_WIDTH), BF16),
            pltpu.VMEM((tm, GMLP_WIDTH), F32),
        ],
        compiler_params=pltpu.CompilerParams(
            dimension_semantics=("parallel",), vmem_limit_bytes=VMEM_LIMIT),
        name="mixer_in",
    )(h, mod_rows, mod_rows, mix_norm, win, wkpe, v_norm, ws, bs_full, q_lat_norm, wuq, kv_lat_norm, wukv,
      qn_g, qr_g, kn_g, kr_g, on_g, pos, freq)


def _attn_kernel(q_ref, k_ref, v_ref, o_ref, vx_scr, s_scr, p_scr, *, sub, nsub):
    vx_scr[:, :V_HEAD] = v_ref[...]
    lane = lax.broadcasted_iota(jnp.int32, (vx_scr.shape[0], LANES), 1)
    vx_scr[:, V_HEAD:] = jnp.where(lane == 0, 1.0, 0.0).astype(BF16)

    def rows(i):
        return pl.ds(pl.multiple_of(i * sub, sub), sub)

    def qk(i, slot):
        s_scr[slot] = lax.dot_general(q_ref[rows(i), :], k_ref[...], (((1,), (1,)), ((), ())),
                                      preferred_element_type=F32)

    def softmax(slot):
        s = s_scr[slot]
        p_scr[slot] = jnp.exp2(s - jnp.max(s, axis=-1, keepdims=True)).astype(BF16)

    def pv(i, slot):
        ox = _dot(p_scr[slot], vx_scr[...])
        o_ref[rows(i), :] = ox[:, :V_HEAD] / ox[:, V_HEAD:V_HEAD + 1]

    def stage(i, slot):
        qk(i + 2, slot)
        softmax(1 - slot)
        pv(i, slot)

    qk(0, 0)
    softmax(0)
    qk(1, 1)

    def stages(j, carry):
        for r in range(ATTN_UNROLL):
            stage(ATTN_UNROLL * j + r, r % 2)
        return carry

    n_loop = (nsub - 2) // ATTN_UNROLL
    lax.fori_loop(0, n_loop, stages, 0)
    for i in range(n_loop * ATTN_UNROLL, nsub - 2):
        stage(i, i % 2)
    softmax(1)
    pv(nsub - 2, 0)
    pv(nsub - 1, 1)


def _attn_call(q, k, v, *, batch, seq, sub=256):
    t = q.shape[0]
    nsub = seq // sub
    assert nsub % 2 == 0 and nsub >= 4
    return pl.pallas_call(
        functools.partial(_attn_kernel, sub=sub, nsub=nsub),
        grid=(batch, MLA_HEADS),
        in_specs=[
            pl.BlockSpec((seq, QK_PAD), lambda b, h: (b, h)),
            pl.BlockSpec((seq, QK_PAD), lambda b, h: (b, h)),
            pl.BlockSpec((seq, V_HEAD), lambda b, h: (b, h)),
        ],
        out_specs=pl.BlockSpec((seq, V_HEAD), lambda b, h: (b, h)),
        out_shape=jax.ShapeDtypeStruct((t, MLA_WIDTH), F32),
        scratch_shapes=[
            pltpu.VMEM((seq, 2 * V_HEAD), BF16),
            pltpu.VMEM((2, sub, seq), F32),
            pltpu.VMEM((2, sub, seq), BF16),
        ],
        compiler_params=pltpu.CompilerParams(
            dimension_semantics=("parallel", "parallel"), vmem_limit_bytes=VMEM_LIMIT),
        name="mla_attention",
    )(q, k, v)


def _outproj_kernel(h_ref, a_ref, m_ref, g_ref, mg_ref, wo_ref, o_ref):
    m_n = _rms_rows(m_ref[...], mg_ref[...]).astype(BF16)
    y = _dot(a_ref[...], wo_ref[:GMLP_WIDTH, :]) + _dot(m_n, wo_ref[GMLP_WIDTH:, :])
    o_ref[...] = h_ref[...] + g_ref[...] * y


def _outproj_call(h, a, m, mod_rows, mla_gain, wo, *, seq, tm=512):
    t, d = h.shape
    blocks_per_batch = seq // tm
    return pl.pallas_call(
        _outproj_kernel,
        grid=(t // tm,),
        in_specs=[
            pl.BlockSpec((tm, d), lambda i: (i, 0)),
            pl.BlockSpec((tm, GMLP_WIDTH), lambda i: (i, 0)),
            pl.BlockSpec((tm, MLA_WIDTH), lambda i: (i, 0)),
            pl.BlockSpec((None, 1, d), lambda i: ((i // blocks_per_batch) * N_MOD + 5, 0, 0)),
            _const_spec(mla_gain.shape), _const_spec(wo.shape),
        ],
        out_specs=pl.BlockSpec((tm, d), lambda i: (i, 0)),
        out_shape=jax.ShapeDtypeStruct((t, d), F32),
        compiler_params=pltpu.CompilerParams(
            dimension_semantics=("parallel",), vmem_limit_bytes=VMEM_LIMIT),
        name="mixer_out",
    )(h, a, m, mod_rows, mla_gain, wo)


def _ffn_weights(w_gate, w_up, w_down):
    return w_gate.astype(BF16), w_up.astype(BF16), w_down.astype(BF16)


def _half_swap(x):
    half = x.shape[-1] // 2
    return jnp.concatenate([x[..., half:], x[..., :half]], axis=-1)


def kernel(x, c, positions, w_ada, b_ada, ffn1_norm, ffn1_w_gate, ffn1_w_up, ffn1_w_down, mix_norm, w_in, gmlp_v_norm, gmlp_w_s, gmlp_b_s, q_lat_norm, w_uq, kv_lat_norm, w_ukv, q_nope_norm, q_rope_norm, k_nope_norm, k_rope_norm, out_norm_gmlp, out_norm_mla, w_out, ffn2_norm, ffn2_w_gate, ffn2_w_up, ffn2_w_down, final_norm):
    batch, seq, d = x.shape
    depth = w_ada.shape[0]
    t = batch * seq

    inv_freq = ROPE_THETA ** (-jnp.arange(0, QK_ROPE, 2, dtype=F32) / QK_ROPE)
    freq = jnp.tile(inv_freq, 4)[None, :]
    pos = positions.reshape(t, 1)
    c_pad = jnp.pad(c, ((0, 8 - batch), (0, 0)))

    def row(v):
        return v[None, :]

    h = x.reshape(t, d)
    for l in range(depth):
        mod = _mod_call(c_pad, w_ada[l], row(b_ada[l]))
        mod_rows = mod[:batch].reshape(batch * N_MOD, 1, d)

        ni, nf = t // FFN_TM, D_FF // FFN_TF
        jobs = [
            (ffn2_w_gate[l], (d // ni, FFN_TF), lambda i, j: (i, j), True),
            (ffn2_w_up[l], (d // ni, FFN_TF), lambda i, j: (i, j), True),
            (ffn2_w_down[l], (FFN_TF, d // ni), lambda i, j: (j, i), True),
            (w_in[l], (d // ni, IN_COLS), lambda i, j: (i, 0), False),
            (w_out[l], (d // ni, d), lambda i, j: (i, 0), False),
        ]
        h, (wg2, wu2, wd2, win, wout) = _ffn_call(
            h, mod_rows, 0, row(ffn1_norm[l]), row(final_norm[l]),
            *_ffn_weights(ffn1_w_gate[l], ffn1_w_up[l], ffn1_w_down[l]),
            seq=seq, final_norm=False, cast_jobs=jobs, tm=FFN_TM, tf=FFN_TF)

        k_rope_w = w_in[l][:, IN_COLS - QK_ROPE:]
        wkpe = jnp.concatenate([k_rope_w, _half_swap(k_rope_w)], axis=1).astype(BF16)
        uq = w_uq[l].reshape(Q_RANK, MLA_HEADS, QK_NOPE + QK_ROPE)
        uq_rope = uq[..., QK_NOPE:]
        wuq = jnp.concatenate([uq[..., :QK_NOPE], uq_rope, _half_swap(uq_rope)], axis=-1)
        wuq = wuq.reshape(Q_RANK, MLA_HEADS * QK_PAD).astype(BF16)
        bs_full = jnp.repeat(gmlp_b_s[l].T, GMLP_GROUP, axis=1)
        a, q, k, v = _mix_call(
            h, mod_rows, row(mix_norm[l]), win, wkpe, row(gmlp_v_norm[l]), gmlp_w_s[l].astype(BF16), bs_full,
            row(q_lat_norm[l]), wuq, row(kv_lat_norm[l]), w_ukv[l].astype(BF16),
            row(q_nope_norm[l]), row(jnp.concatenate([q_rope_norm[l], _half_swap(q_rope_norm[l])])),
            row(k_nope_norm[l]), row(jnp.concatenate([k_rope_norm[l], _half_swap(k_rope_norm[l])])),
            row(out_norm_gmlp[l]), pos, freq, seq=seq)

        m = _attn_call(q, k, v, batch=batch, seq=seq)
        h = _outproj_call(h, a, m, mod_rows, row(out_norm_mla[l]), wout, seq=seq)

        h, _ = _ffn_call(h, mod_rows, 6, row(ffn2_norm[l]), row(final_norm[l]), wg2, wu2, wd2,
                         seq=seq, final_norm=True, tm=FFN_TM, tf=FFN_TF)
    return h.reshape(batch, seq, d)
```

```python
import functools

import jax
import jax.numpy as jnp
from jax import lax
from jax.experimental import pallas as pl
from jax.experimental.pallas import tpu as pltpu

D_MODEL = 2048
GMLP_WIDTH = 1024
GMLP_GROUP = 128
GMLP_HEADS = GMLP_WIDTH // GMLP_GROUP
CHUNK = 128
V_HEAD = 128
MLA_WIDTH = 1024
MLA_HEADS = MLA_WIDTH // V_HEAD
QK_NOPE = 128
QK_ROPE = 64
Q_RANK = D_MODEL // 4
KV_RANK = D_MODEL // 8
IN_COLS = 2 * GMLP_WIDTH + Q_RANK + KV_RANK + QK_ROPE
D_FF = 5632
ROPE_THETA = 10000.0
EPS = 1e-6
N_MOD = 9
LOG2_E = 1.4426950408889634

LANES = 128
QK_PAD = 2 * LANES
IN_COLS_EXT = IN_COLS + QK_ROPE
VMEM_LIMIT = 56 * 1024 * 1024
ATTN_UNROLL = 4
FFN_TM = 512
FFN_TF = 512
FFN_EDGE_SPLIT = 2

F32 = jnp.float32
BF16 = jnp.bfloat16


def _dot(a, b):
    return jnp.dot(a, b, preferred_element_type=F32)


def _dot_nt(a, b):
    return lax.dot_general(a, b, (((1,), (1,)), ((), ())), preferred_element_type=F32)


def _rms_rows(x, gain):
    return x * lax.rsqrt(jnp.mean(x * x, axis=-1, keepdims=True) + EPS) * gain


def _const_spec(shape):
    zeros = (0,) * len(shape)
    return pl.BlockSpec(shape, lambda *_: zeros, pipeline_mode=pl.Buffered(1))


def _mod_kernel(c_ref, w_ref, b_ref, o_ref):
    c = c_ref[...]
    c_act = (c * jax.nn.sigmoid(c)).astype(BF16)
    o_ref[...] = _dot(c_act, w_ref[...].astype(BF16)) + b_ref[...]


def _mod_call(c_pad, w_ada, b_ada, *, tn=1024):
    rows, d = c_pad.shape
    n = w_ada.shape[1]
    return pl.pallas_call(
        _mod_kernel,
        grid=(n // tn,),
        in_specs=[
            pl.BlockSpec((rows, d), lambda j: (0, 0)),
            pl.BlockSpec((d, tn), lambda j: (0, j)),
            pl.BlockSpec((1, tn), lambda j: (0, j)),
        ],
        out_specs=pl.BlockSpec((rows, tn), lambda j: (0, j)),
        out_shape=jax.ShapeDtypeStruct((rows, n), F32),
        compiler_params=pltpu.CompilerParams(
            dimension_semantics=("arbitrary",), vmem_limit_bytes=VMEM_LIMIT),
        name="adaln_mod",
    )(c_pad, w_ada, b_ada)


def _ffn_kernel(x_ref, sh_ref, sc_ref, g_ref, nw_ref, fin_ref, wg_ref, wu_ref, wd_ref, *rest,
                final_norm, cast_every_step):
    n_cast = len(cast_every_step)
    cast_in = rest[:n_cast]
    o_ref, *cast_out = rest[n_cast:2 * n_cast + 1]
    n_scr, acc_scr = rest[2 * n_cast + 1:]
    j = pl.program_id(1)

    def cast(every_step):
        for src, dst, flag in zip(cast_in, cast_out, cast_every_step):
            if flag == every_step:
                dst[...] = src[...].astype(BF16)

    cast(True)

    def step(first, last, n_split):
        sub = x_ref.shape[0] // n_split
        for r in range(n_split):
            rows = slice(r * sub, (r + 1) * sub)
            if first:
                gain = nw_ref[...] * (1.0 + sc_ref[...])
                n = (_rms_rows(x_ref[rows, :], gain) + sh_ref[...]).astype(BF16)
                n_scr[rows, :] = n
            else:
                n = n_scr[rows, :]
            hg = _dot(n, wg_ref[...])
            hu = _dot(n, wu_ref[...])
            hid = (hg * jax.nn.sigmoid(hg) * hu).astype(BF16)
            y = _dot(hid, wd_ref[...])
            if not first:
                y = acc_scr[rows, :] + y
            if last:
                h = x_ref[rows, :] + (0.5 * g_ref[...]) * y
                if final_norm:
                    h = _rms_rows(h, fin_ref[...])
                o_ref[rows, :] = h
            else:
                acc_scr[rows, :] = y

    last_j = pl.num_programs(1) - 1

    @pl.when(j == 0)
    def _():
        cast(False)
        step(True, False, FFN_EDGE_SPLIT)

    @pl.when(jnp.logical_and(j > 0, j < last_j))
    def _():
        step(False, False, 1)

    @pl.when(j == last_j)
    def _():
        step(False, True, FFN_EDGE_SPLIT)


def _ffn_call(x, mod_rows, mod_base, norm_w, fin_w, wg, wu, wd, *, seq, final_norm, cast_jobs=(),
              tm=512, tf=512):
    t, d = x.shape
    nf = wg.shape[1] // tf
    blocks_per_batch = seq // tm

    def row(k):
        return pl.BlockSpec((None, 1, d), lambda i, j: ((i // blocks_per_batch) * N_MOD + mod_base + k, 0, 0))

    cast_arrays = [job[0] for job in cast_jobs]
    cast_specs = [pl.BlockSpec(job[1], job[2]) for job in cast_jobs]
    outs = pl.pallas_call(
        functools.partial(_ffn_kernel, final_norm=final_norm,
                          cast_every_step=tuple(job[3] for job in cast_jobs)),
        grid=(t // tm, nf),
        in_specs=[
            pl.BlockSpec((tm, d), lambda i, j: (i, 0)),
            row(0), row(1), row(2),
            pl.BlockSpec((1, d), lambda i, j: (0, 0)),
            pl.BlockSpec((1, d), lambda i, j: (0, 0)),
            pl.BlockSpec((d, tf), lambda i, j: (0, j)),
            pl.BlockSpec((d, tf), lambda i, j: (0, j)),
            pl.BlockSpec((tf, d), lambda i, j: (j, 0)),
            *cast_specs,
        ],
        out_specs=[pl.BlockSpec((tm, d), lambda i, j: (i, 0)), *cast_specs],
        out_shape=[jax.ShapeDtypeStruct((t, d), F32),
                   *(jax.ShapeDtypeStruct(a.shape, BF16) for a in cast_arrays)],
        scratch_shapes=[pltpu.VMEM((tm, d), BF16), pltpu.VMEM((tm, d), F32)],
        compiler_params=pltpu.CompilerParams(
            dimension_semantics=("parallel", "arbitrary"), vmem_limit_bytes=VMEM_LIMIT),
        name="swiglu_ffn",
    )(x, mod_rows, mod_rows, mod_rows, norm_w, fin_w, wg, wu, wd, *cast_arrays)
    return outs[0], outs[1:]


def _mix_kernel(h_ref, sh_ref, sc_ref, nw_ref, win_ref, wkpe_ref, vn_ref, ws_ref, bs_ref,
                qln_ref, wuq_ref, kvn_ref, wukv_ref, qng_ref, qrg_ref, kng_ref, krg_ref,
                ong_ref, pos_ref, freq_ref,
                a_ref, q_ref, k_ref, v_ref,
                u_scr, v_scr, a_scr, *, tm):
    n = (_rms_rows(h_ref[...], nw_ref[...] * (1.0 + sc_ref[...])) + sh_ref[...]).astype(BF16)

    o1 = GMLP_WIDTH
    o2 = 2 * GMLP_WIDTH
    o3 = o2 + Q_RANK
    o4 = o3 + KV_RANK
    u_scr[...] = jax.nn.gelu(_dot_nt(n, win_ref[:o1, :]))
    vv = jax.nn.gelu(_dot_nt(n, win_ref[o1:o2, :]))
    v_scr[...] = _rms_rows(vv, vn_ref[...]).astype(BF16)
    for c in range(tm // CHUNK):
        rows = slice(c * CHUNK, (c + 1) * CHUNK)
        for g in range(GMLP_HEADS):
            cols = slice(g * GMLP_GROUP, (g + 1) * GMLP_GROUP)
            mixed = _dot(ws_ref[g], v_scr[rows, cols]) + bs_ref[:, cols]
            a_scr[rows, cols] = u_scr[rows, cols] * mixed
    a_ref[...] = _rms_rows(a_scr[...], ong_ref[...]).astype(BF16)

    ang = pos_ref[...].astype(F32) * freq_ref[...]
    lane = lax.broadcasted_iota(jnp.int32, ang.shape, 1)
    sin = jnp.sin(ang)
    rot = jnp.where(lane < QK_ROPE, jnp.cos(ang), jnp.where(lane < QK_ROPE + QK_ROPE // 2, -sin, sin))

    def rope(z, gain):
        w = _rms_rows(z, gain) * rot
        return w + pltpu.roll(w, QK_ROPE, axis=1)

    scale = (QK_NOPE + QK_ROPE) ** -0.5 * LOG2_E
    q_lat = _rms_rows(_dot_nt(n, win_ref[o2:o3, :]), qln_ref[...]).astype(BF16)
    for hd in range(MLA_HEADS):
        base = hd * QK_PAD
        qq = _dot(q_lat, wuq_ref[:, base:base + QK_PAD])
        q_ref[:, base:base + LANES] = (_rms_rows(qq[:, :LANES], qng_ref[...]) * scale).astype(BF16)
        q_ref[:, base + LANES:base + QK_PAD] = (rope(qq[:, LANES:], qrg_ref[...]) * scale).astype(BF16)

    k_pe = rope(_dot_nt(n, wkpe_ref[...]), krg_ref[...])
    k_pe = jnp.where(lane < QK_ROPE, k_pe, 0.0).astype(BF16)
    kv_lat = _rms_rows(_dot_nt(n, win_ref[o3:o4, :]), kvn_ref[...]).astype(BF16)
    for hd in range(MLA_HEADS):
        base = hd * QK_PAD
        kv = _dot(kv_lat, wukv_ref[:, base:base + QK_PAD])
        k_ref[:, base:base + LANES] = _rms_rows(kv[:, :LANES], kng_ref[...]).astype(BF16)
        k_ref[:, base + LANES:base + QK_PAD] = k_pe
        v_ref[:, hd * V_HEAD:(hd + 1) * V_HEAD] = kv[:, LANES:].astype(BF16)


def _mix_call(h, mod_rows, mix_norm, win, wkpe, v_norm, ws, bs_full, q_lat_norm, wuq, kv_lat_norm, wukv,
              qn_g, qr_g, kn_g, kr_g, on_g, pos, freq, *, seq, tm=512):
    t, d = h.shape
    blocks_per_batch = seq // tm

    def row(k):
        return pl.BlockSpec((None, 1, d), lambda i: ((i // blocks_per_batch) * N_MOD + 3 + k, 0, 0))

    def tok(width):
        return pl.BlockSpec((tm, width), lambda i: (i, 0))

    return pl.pallas_call(
        functools.partial(_mix_kernel, tm=tm),
        grid=(t // tm,),
        in_specs=[
            tok(d), row(0), row(1),
            _const_spec(mix_norm.shape), _const_spec(win.shape), _const_spec(wkpe.shape),
            _const_spec(v_norm.shape),
            _const_spec(ws.shape), _const_spec(bs_full.shape),
            _const_spec(q_lat_norm.shape), _const_spec(wuq.shape),
            _const_spec(kv_lat_norm.shape), _const_spec(wukv.shape),
            _const_spec(qn_g.shape), _const_spec(qr_g.shape), _const_spec(kn_g.shape),
            _const_spec(kr_g.shape), _const_spec(on_g.shape),
            tok(1), _const_spec(freq.shape),
        ],
        out_specs=[tok(GMLP_WIDTH), tok(MLA_HEADS * QK_PAD), tok(MLA_HEADS * QK_PAD), tok(MLA_WIDTH)],
        out_shape=[
            jax.ShapeDtypeStruct((t, GMLP_WIDTH), BF16),
            jax.ShapeDtypeStruct((t, MLA_HEADS * QK_PAD), BF16),
            jax.ShapeDtypeStruct((t, MLA_HEADS * QK_PAD), BF16),
            jax.ShapeDtypeStruct((t, MLA_WIDTH), BF16),
        ],
        scratch_shapes=[
            pltpu.VMEM((tm, GMLP_WIDTH), F32),
            pltpu.VMEM((tm, GMLP_WIDTH), BF16),
            pltpu.VMEM((tm, GMLP_WIDTH), F32),
        ],
        compiler_params=pltpu.CompilerParams(
            dimension_semantics=("parallel",), vmem_limit_bytes=VMEM_LIMIT),
        name="mixer_in",
    )(h, mod_rows, mod_rows, mix_norm, win, wkpe, v_norm, ws, bs_full, q_lat_norm, wuq, kv_lat_norm, wukv,
      qn_g, qr_g, kn_g, kr_g, on_g, pos, freq)


def _attn_kernel(q_ref, k_ref, v_ref, o_ref, vx_scr, s_scr, p_scr, *, sub, nsub):
    vx_scr[:, :V_HEAD] = v_ref[...]
    lane = lax.broadcasted_iota(jnp.int32, (vx_scr.shape[0], LANES), 1)
    vx_scr[:, V_HEAD:] = jnp.where(lane == 0, 1.0, 0.0).astype(BF16)

    def rows(i):
        return pl.ds(pl.multiple_of(i * sub, sub), sub)

    def qk(i, slot):
        s_scr[slot] = lax.dot_general(q_ref[rows(i), :], k_ref[...], (((1,), (1,)), ((), ())),
                                      preferred_element_type=F32)

    def softmax(slot):
        s = s_scr[slot]
        p_scr[slot] = jnp.exp2(s - jnp.max(s, axis=-1, keepdims=True)).astype(BF16)

    def pv(i, slot):
        ox = _dot(p_scr[slot], vx_scr[...])
        o_ref[rows(i), :] = ox[:, :V_HEAD] / ox[:, V_HEAD:V_HEAD + 1]

    def stage(i, slot):
        qk(i + 2, slot)
        softmax(1 - slot)
        pv(i, slot)

    qk(0, 0)
    softmax(0)
    qk(1, 1)

    def stages(j, carry):
        for r in range(ATTN_UNROLL):
            stage(ATTN_UNROLL * j + r, r % 2)
        return carry

    n_loop = (nsub - 2) // ATTN_UNROLL
    lax.fori_loop(0, n_loop, stages, 0)
    for i in range(n_loop * ATTN_UNROLL, nsub - 2):
        stage(i, i % 2)
    softmax(1)
    pv(nsub - 2, 0)
    pv(nsub - 1, 1)


def _attn_call(q, k, v, *, batch, seq, sub=256):
    t = q.shape[0]
    nsub = seq // sub
    assert nsub % 2 == 0 and nsub >= 4
    return pl.pallas_call(
        functools.partial(_attn_kernel, sub=sub, nsub=nsub),
        grid=(batch, MLA_HEADS),
        in_specs=[
            pl.BlockSpec((seq, QK_PAD), lambda b, h: (b, h)),
            pl.BlockSpec((seq, QK_PAD), lambda b, h: (b, h)),
            pl.BlockSpec((seq, V_HEAD), lambda b, h: (b, h)),
        ],
        out_specs=pl.BlockSpec((seq, V_HEAD), lambda b, h: (b, h)),
        out_shape=jax.ShapeDtypeStruct((t, MLA_WIDTH), F32),
        scratch_shapes=[
            pltpu.VMEM((seq, 2 * V_HEAD), BF16),
            pltpu.VMEM((2, sub, seq), F32),
            pltpu.VMEM((2, sub, seq), BF16),
        ],
        compiler_params=pltpu.CompilerParams(
            dimension_semantics=("parallel", "parallel"), vmem_limit_bytes=VMEM_LIMIT),
        name="mla_attention",
    )(q, k, v)


def _outproj_kernel(h_ref, a_ref, m_ref, g_ref, mg_ref, wo_ref, o_ref):
    m_n = _rms_rows(m_ref[...], mg_ref[...]).astype(BF16)
    y = _dot(a_ref[...], wo_ref[:GMLP_WIDTH, :]) + _dot(m_n, wo_ref[GMLP_WIDTH:, :])
    o_ref[...] = h_ref[...] + g_ref[...] * y


def _outproj_call(h, a, m, mod_rows, mla_gain, wo, *, seq, tm=512):
    t, d = h.shape
    blocks_per_batch = seq // tm
    return pl.pallas_call(
        _outproj_kernel,
        grid=(t // tm,),
        in_specs=[
            pl.BlockSpec((tm, d), lambda i: (i, 0)),
            pl.BlockSpec((tm, GMLP_WIDTH), lambda i: (i, 0)),
            pl.BlockSpec((tm, MLA_WIDTH), lambda i: (i, 0)),
            pl.BlockSpec((None, 1, d), lambda i: ((i // blocks_per_batch) * N_MOD + 5, 0, 0)),
            _const_spec(mla_gain.shape), _const_spec(wo.shape),
        ],
        out_specs=pl.BlockSpec((tm, d), lambda i: (i, 0)),
        out_shape=jax.ShapeDtypeStruct((t, d), F32),
        compiler_params=pltpu.CompilerParams(
            dimension_semantics=("parallel",), vmem_limit_bytes=VMEM_LIMIT),
        name="mixer_out",
    )(h, a, m, mod_rows, mla_gain, wo)


def _ffn_weights(w_gate, w_up, w_down):
    return w_gate.astype(BF16), w_up.astype(BF16), w_down.astype(BF16)


def _half_swap(x):
    half = x.shape[-1] // 2
    return jnp.concatenate([x[..., half:], x[..., :half]], axis=-1)


def kernel(x, c, positions, w_ada, b_ada, ffn1_norm, ffn1_w_gate, ffn1_w_up, ffn1_w_down, mix_norm, w_in, gmlp_v_norm, gmlp_w_s, gmlp_b_s, q_lat_norm, w_uq, kv_lat_norm, w_ukv, q_nope_norm, q_rope_norm, k_nope_norm, k_rope_norm, out_norm_gmlp, out_norm_mla, w_out, ffn2_norm, ffn2_w_gate, ffn2_w_up, ffn2_w_down, final_norm):
    batch, seq, d = x.shape
    depth = w_ada.shape[0]
    t = batch * seq

    inv_freq = ROPE_THETA ** (-jnp.arange(0, QK_ROPE, 2, dtype=F32) / QK_ROPE)
    freq = jnp.tile(inv_freq, 4)[None, :]
    pos = positions.reshape(t, 1)
    c_pad = jnp.pad(c, ((0, 8 - batch), (0, 0)))

    def row(v):
        return v[None, :]

    h = x.reshape(t, d)
    for l in range(depth):
        mod = _mod_call(c_pad, w_ada[l], row(b_ada[l]))
        mod_rows = mod[:batch].reshape(batch * N_MOD, 1, d)

        w_in_t = w_in[l].T
        ni, nf = t // FFN_TM, D_FF // FFN_TF
        jobs = [
            (ffn2_w_gate[l], (d // ni, FFN_TF), lambda i, j: (i, j), True),
            (ffn2_w_up[l], (d // ni, FFN_TF), lambda i, j: (i, j), True),
            (ffn2_w_down[l], (FFN_TF, d // ni), lambda i, j: (j, i), True),
            (w_in_t, (IN_COLS, d // ni), lambda i, j: (0, i), False),
            (w_out[l], (d // ni, d), lambda i, j: (i, 0), False),
        ]
        h, (wg2, wu2, wd2, win, wout) = _ffn_call(
            h, mod_rows, 0, row(ffn1_norm[l]), row(final_norm[l]),
            *_ffn_weights(ffn1_w_gate[l], ffn1_w_up[l], ffn1_w_down[l]),
            seq=seq, final_norm=False, cast_jobs=jobs, tm=FFN_TM, tf=FFN_TF)

        k_rope_w = w_in_t[IN_COLS - QK_ROPE:]
        half = QK_ROPE // 2
        wkpe = jnp.concatenate([k_rope_w, k_rope_w[half:], k_rope_w[:half]], axis=0).astype(BF16)
        uq = w_uq[l].reshape(Q_RANK, MLA_HEADS, QK_NOPE + QK_ROPE)
        uq_rope = uq[..., QK_NOPE:]
        wuq = jnp.concatenate([uq[..., :QK_NOPE], uq_rope, _half_swap(uq_rope)], axis=-1)
        wuq = wuq.reshape(Q_RANK, MLA_HEADS * QK_PAD).astype(BF16)
        bs_full = jnp.repeat(gmlp_b_s[l].T, GMLP_GROUP, axis=1)
        a, q, k, v = _mix_call(
            h, mod_rows, row(mix_norm[l]), win, wkpe, row(gmlp_v_norm[l]), gmlp_w_s[l].astype(BF16), bs_full,
            row(q_lat_norm[l]), wuq, row(kv_lat_norm[l]), w_ukv[l].astype(BF16),
            row(q_nope_norm[l]), row(jnp.concatenate([q_rope_norm[l], _half_swap(q_rope_norm[l])])),
            row(k_nope_norm[l]), row(jnp.concatenate([k_rope_norm[l], _half_swap(k_rope_norm[l])])),
            row(out_norm_gmlp[l]), pos, freq, seq=seq)

        m = _attn_call(q, k, v, batch=batch, seq=seq)
        h = _outproj_call(h, a, m, mod_rows, row(out_norm_mla[l]), wout, seq=seq)

        h, _ = _ffn_call(h, mod_rows, 6, row(ffn2_norm[l]), row(final_norm[l]), wg2, wu2, wd2,
                         seq=seq, final_norm=True, tm=FFN_TM, tf=FFN_TF)
    return h.reshape(batch, seq, d)
```

```python
import functools

import jax
import jax.numpy as jnp
from jax import lax
from jax.experimental import pallas as pl
from jax.experimental.pallas import tpu as pltpu

D_MODEL = 2048
GMLP_WIDTH = 1024
GMLP_GROUP = 128
GMLP_HEADS = GMLP_WIDTH // GMLP_GROUP
CHUNK = 128
V_HEAD = 128
MLA_WIDTH = 1024
MLA_HEADS = MLA_WIDTH // V_HEAD
QK_NOPE = 128
QK_ROPE = 64
Q_RANK = D_MODEL // 4
KV_RANK = D_MODEL // 8
IN_COLS = 2 * GMLP_WIDTH + Q_RANK + KV_RANK + QK_ROPE
D_FF = 5632
ROPE_THETA = 10000.0
EPS = 1e-6
N_MOD = 9
LOG2_E = 1.4426950408889634

LANES = 128
QK_PAD = 2 * LANES
IN_COLS_EXT = IN_COLS + QK_ROPE
VMEM_LIMIT = 56 * 1024 * 1024
BF16_SUBLANES = 16
ATTN_SUB = 512
FFN_TM = 512
FFN_TF = 512
FFN_EDGE_SPLIT = 2

F32 = jnp.float32
BF16 = jnp.bfloat16


def _dot(a, b):
    return jnp.dot(a, b, preferred_element_type=F32)


def _dot_nt(a, b):
    return lax.dot_general(a, b, (((1,), (1,)), ((), ())), preferred_element_type=F32)


def _rms_rows(x, gain):
    return x * lax.rsqrt(jnp.mean(x * x, axis=-1, keepdims=True) + EPS) * gain


def _const_spec(shape):
    zeros = (0,) * len(shape)
    return pl.BlockSpec(shape, lambda *_: zeros, pipeline_mode=pl.Buffered(1))


def _mod_kernel(c_ref, w_ref, b_ref, o_ref):
    c = c_ref[...]
    c_act = (c * jax.nn.sigmoid(c)).astype(BF16)
    o_ref[...] = _dot(c_act, w_ref[...].astype(BF16)) + b_ref[...]


def _mod_call(c_pad, w_ada, b_ada, *, tn=1024):
    rows, d = c_pad.shape
    n = w_ada.shape[1]
    return pl.pallas_call(
        _mod_kernel,
        grid=(n // tn,),
        in_specs=[
            pl.BlockSpec((rows, d), lambda j: (0, 0)),
            pl.BlockSpec((d, tn), lambda j: (0, j)),
            pl.BlockSpec((1, tn), lambda j: (0, j)),
        ],
        out_specs=pl.BlockSpec((rows, tn), lambda j: (0, j)),
        out_shape=jax.ShapeDtypeStruct((rows, n), F32),
        compiler_params=pltpu.CompilerParams(
            dimension_semantics=("arbitrary",), vmem_limit_bytes=VMEM_LIMIT),
        name="adaln_mod",
    )(c_pad, w_ada, b_ada)


def _ffn_kernel(x_ref, sh_ref, sc_ref, g_ref, nw_ref, fin_ref, wg_ref, wu_ref, wd_ref, *rest,
                final_norm, cast_every_step):
    n_cast = len(cast_every_step)
    cast_in = rest[:n_cast]
    o_ref, *cast_out = rest[n_cast:2 * n_cast + 1]
    n_scr, acc_scr = rest[2 * n_cast + 1:]
    j = pl.program_id(1)

    def cast(every_step):
        for src, dst, flag in zip(cast_in, cast_out, cast_every_step):
            if flag == every_step:
                dst[...] = src[...].astype(BF16)

    cast(True)

    def step(first, last, n_split):
        sub = x_ref.shape[0] // n_split
        for r in range(n_split):
            rows = slice(r * sub, (r + 1) * sub)
            if first:
                gain = nw_ref[...] * (1.0 + sc_ref[...])
                n = (_rms_rows(x_ref[rows, :], gain) + sh_ref[...]).astype(BF16)
                n_scr[rows, :] = n
            else:
                n = n_scr[rows, :]
            hg = _dot(n, wg_ref[...])
            hu = _dot(n, wu_ref[...])
            hid = (hg * jax.nn.sigmoid(hg) * hu).astype(BF16)
            y = _dot(hid, wd_ref[...])
            if not first:
                y = acc_scr[rows, :] + y
            if last:
                h = x_ref[rows, :] + (0.5 * g_ref[...]) * y
                if final_norm:
                    h = _rms_rows(h, fin_ref[...])
                o_ref[rows, :] = h
            else:
                acc_scr[rows, :] = y

    last_j = pl.num_programs(1) - 1

    @pl.when(j == 0)
    def _():
        cast(False)
        step(True, False, FFN_EDGE_SPLIT)

    @pl.when(jnp.logical_and(j > 0, j < last_j))
    def _():
        step(False, False, 1)

    @pl.when(j == last_j)
    def _():
        step(False, True, FFN_EDGE_SPLIT)


def _ffn_call(x, mod_rows, mod_base, norm_w, fin_w, wg, wu, wd, *, seq, final_norm, cast_jobs=(),
              tm=512, tf=512):
    t, d = x.shape
    nf = wg.shape[1] // tf
    blocks_per_batch = seq // tm

    def row(k):
        return pl.BlockSpec((None, 1, d), lambda i, j: ((i // blocks_per_batch) * N_MOD + mod_base + k, 0, 0))

    cast_arrays = [job[0] for job in cast_jobs]
    cast_specs = [pl.BlockSpec(job[1], job[2]) for job in cast_jobs]
    outs = pl.pallas_call(
        functools.partial(_ffn_kernel, final_norm=final_norm,
                          cast_every_step=tuple(job[3] for job in cast_jobs)),
        grid=(t // tm, nf),
        in_specs=[
            pl.BlockSpec((tm, d), lambda i, j: (i, 0)),
            row(0), row(1), row(2),
            pl.BlockSpec((1, d), lambda i, j: (0, 0)),
            pl.BlockSpec((1, d), lambda i, j: (0, 0)),
            pl.BlockSpec((d, tf), lambda i, j: (0, j)),
            pl.BlockSpec((d, tf), lambda i, j: (0, j)),
            pl.BlockSpec((tf, d), lambda i, j: (j, 0)),
            *cast_specs,
        ],
        out_specs=[pl.BlockSpec((tm, d), lambda i, j: (i, 0)), *cast_specs],
        out_shape=[jax.ShapeDtypeStruct((t, d), F32),
                   *(jax.ShapeDtypeStruct(a.shape, BF16) for a in cast_arrays)],
        scratch_shapes=[pltpu.VMEM((tm, d), BF16), pltpu.VMEM((tm, d), F32)],
        compiler_params=pltpu.CompilerParams(
            dimension_semantics=("parallel", "arbitrary"), vmem_limit_bytes=VMEM_LIMIT),
        name="swiglu_ffn",
    )(x, mod_rows, mod_rows, mod_rows, norm_w, fin_w, wg, wu, wd, *cast_arrays)
    return outs[0], outs[1:]


def _mix_kernel(h_ref, sh_ref, sc_ref, nw_ref, win_ref, wkpe_ref, vn_ref, ws_ref, bs_ref,
                qln_ref, wuq_ref, kvn_ref, wk_ref, wvt_ref, qng_ref, qrg_ref, kng_ref, krg_ref,
                ong_ref, pos_ref, freq_ref,
                a_ref, q_ref, k_ref, vt_ref,
                u_scr, v_scr, a_scr, *, tm):
    n = (_rms_rows(h_ref[...], nw_ref[...] * (1.0 + sc_ref[...])) + sh_ref[...]).astype(BF16)

    o1 = GMLP_WIDTH
    o2 = 2 * GMLP_WIDTH
    o3 = o2 + Q_RANK
    o4 = o3 + KV_RANK
    u_scr[...] = jax.nn.gelu(_dot_nt(n, win_ref[:o1, :]))
    vv = jax.nn.gelu(_dot_nt(n, win_ref[o1:o2, :]))
    v_scr[...] = _rms_rows(vv, vn_ref[...]).astype(BF16)
    for c in range(tm // CHUNK):
        rows = slice(c * CHUNK, (c + 1) * CHUNK)
        for g in range(GMLP_HEADS):
            cols = slice(g * GMLP_GROUP, (g + 1) * GMLP_GROUP)
            mixed = _dot(ws_ref[g], v_scr[rows, cols]) + bs_ref[:, cols]
            a_scr[rows, cols] = u_scr[rows, cols] * mixed
    a_ref[...] = _rms_rows(a_scr[...], ong_ref[...]).astype(BF16)

    ang = pos_ref[...].astype(F32) * freq_ref[...]
    lane = lax.broadcasted_iota(jnp.int32, ang.shape, 1)
    sin = jnp.sin(ang)
    rot = jnp.where(lane < QK_ROPE, jnp.cos(ang), jnp.where(lane < QK_ROPE + QK_ROPE // 2, -sin, sin))

    def rope(z, gain):
        w = _rms_rows(z, gain) * rot
        return w + pltpu.roll(w, QK_ROPE, axis=1)

    scale = (QK_NOPE + QK_ROPE) ** -0.5 * LOG2_E
    q_lat = _rms_rows(_dot_nt(n, win_ref[o2:o3, :]), qln_ref[...]).astype(BF16)
    for hd in range(MLA_HEADS):
        base = hd * QK_PAD
        qq = _dot(q_lat, wuq_ref[:, base:base + QK_PAD])
        q_ref[:, base:base + LANES] = (_rms_rows(qq[:, :LANES], qng_ref[...]) * scale).astype(BF16)
        q_ref[:, base + LANES:base + QK_PAD] = (rope(qq[:, LANES:], qrg_ref[...]) * scale).astype(BF16)

    k_pe = rope(_dot_nt(n, wkpe_ref[...]), krg_ref[...])
    k_pe = jnp.where(lane < QK_ROPE, k_pe, 0.0).astype(BF16)
    kv_lat = _rms_rows(_dot_nt(n, win_ref[o3:o4, :]), kvn_ref[...]).astype(BF16)
    k_nope = _dot(kv_lat, wk_ref[...])
    for hd in range(MLA_HEADS):
        base = hd * QK_PAD
        cols = slice(hd * QK_NOPE, (hd + 1) * QK_NOPE)
        k_ref[:, base:base + LANES] = _rms_rows(k_nope[:, cols], kng_ref[...]).astype(BF16)
        k_ref[:, base + LANES:base + QK_PAD] = k_pe
    vt_ref[...] = _dot_nt(wvt_ref[...], kv_lat).astype(BF16)


def _mix_call(h, mod_rows, mix_norm, win, wkpe, v_norm, ws, bs_full, q_lat_norm, wuq, kv_lat_norm, wk, wvt,
              qn_g, qr_g, kn_g, kr_g, on_g, pos, freq, *, seq, tm=512):
    t, d = h.shape
    blocks_per_batch = seq // tm
    batch = t // seq

    def row(k):
        return pl.BlockSpec((None, 1, d), lambda i: ((i // blocks_per_batch) * N_MOD + 3 + k, 0, 0))

    def tok(width):
        return pl.BlockSpec((tm, width), lambda i: (i, 0))

    return pl.pallas_call(
        functools.partial(_mix_kernel, tm=tm),
        grid=(t // tm,),
        in_specs=[
            tok(d), row(0), row(1),
            _const_spec(mix_norm.shape), _const_spec(win.shape), _const_spec(wkpe.shape),
            _const_spec(v_norm.shape),
            _const_spec(ws.shape), _const_spec(bs_full.shape),
            _const_spec(q_lat_norm.shape), _const_spec(wuq.shape),
            _const_spec(kv_lat_norm.shape), _const_spec(wk.shape), _const_spec(wvt.shape),
            _const_spec(qn_g.shape), _const_spec(qr_g.shape), _const_spec(kn_g.shape),
            _const_spec(kr_g.shape), _const_spec(on_g.shape),
            tok(1), _const_spec(freq.shape),
        ],
        out_specs=[
            tok(GMLP_WIDTH), tok(MLA_HEADS * QK_PAD), tok(MLA_HEADS * QK_PAD),
            pl.BlockSpec((MLA_WIDTH, tm), lambda i: (i // blocks_per_batch, i % blocks_per_batch)),
        ],
        out_shape=[
            jax.ShapeDtypeStruct((t, GMLP_WIDTH), BF16),
            jax.ShapeDtypeStruct((t, MLA_HEADS * QK_PAD), BF16),
            jax.ShapeDtypeStruct((t, MLA_HEADS * QK_PAD), BF16),
            jax.ShapeDtypeStruct((batch * MLA_WIDTH, seq), BF16),
        ],
        scratch_shapes=[
            pltpu.VMEM((tm, GMLP_WIDTH), F32),
            pltpu.VMEM((tm, GMLP_WIDTH), BF16),
            pltpu.VMEM((tm, GMLP_WIDTH), F32),
        ],
        compiler_params=pltpu.CompilerParams(
            dimension_semantics=("parallel",), vmem_limit_bytes=VMEM_LIMIT),
        name="mixer_in",
    )(h, mod_rows, mod_rows, mix_norm, win, wkpe, v_norm, ws, bs_full, q_lat_norm, wuq, kv_lat_norm, wk, wvt,
      qn_g, qr_g, kn_g, kr_g, on_g, pos, freq)


def _attn_kernel(q_ref, k_ref, vt_ref, o_ref, vx_scr, s_scr, p_scr, *, sub, nsub):
    vx_scr[:V_HEAD, :] = vt_ref[...]
    vx_scr[V_HEAD:, :] = jnp.ones((vx_scr.shape[0] - V_HEAD, vx_scr.shape[1]), BF16)

    def rows(i):
        return slice(i * sub, (i + 1) * sub)

    def qk(i, slot):
        s_scr[slot] = _dot_nt(k_ref[...], q_ref[rows(i), :])

    def softmax(slot):
        s = s_scr[slot]
        p_scr[slot] = jnp.exp2(s - jnp.max(s, axis=0, keepdims=True)).astype(BF16)

    def pv(i, slot):
        ox = _dot(vx_scr[...], p_scr[slot])
        o_ref[rows(i), :] = (ox[:V_HEAD, :] / ox[V_HEAD:V_HEAD + 1, :]).T

    qk(0, 0)
    softmax(0)
    qk(1, 1)
    for i in range(nsub - 2):
        qk(i + 2, i % 2)
        softmax(1 - i % 2)
        pv(i, i % 2)
    softmax(1 - nsub % 2)
    pv(nsub - 2, nsub % 2)
    pv(nsub - 1, 1 - nsub % 2)


def _attn_call(q, k, vt, *, batch, seq, sub=ATTN_SUB):
    t = q.shape[0]
    nsub = seq // sub
    assert nsub >= 3
    return pl.pallas_call(
        functools.partial(_attn_kernel, sub=sub, nsub=nsub),
        grid=(batch, MLA_HEADS),
        in_specs=[
            pl.BlockSpec((seq, QK_PAD), lambda b, h: (b, h)),
            pl.BlockSpec((seq, QK_PAD), lambda b, h: (b, h)),
            pl.BlockSpec((V_HEAD, seq), lambda b, h: (b * MLA_HEADS + h, 0)),
        ],
        out_specs=pl.BlockSpec((seq, V_HEAD), lambda b, h: (b, h)),
        out_shape=jax.ShapeDtypeStruct((t, MLA_WIDTH), F32),
        scratch_shapes=[
            pltpu.VMEM((V_HEAD + BF16_SUBLANES, seq), BF16),
            pltpu.VMEM((2, seq, sub), F32),
            pltpu.VMEM((2, seq, sub), BF16),
        ],
        compiler_params=pltpu.CompilerParams(
            dimension_semantics=("parallel", "parallel"), vmem_limit_bytes=VMEM_LIMIT),
        name="mla_attention",
    )(q, k, vt)


def _outproj_kernel(h_ref, a_ref, m_ref, g_ref, mg_ref, wo_ref, o_ref):
    m_n = _rms_rows(m_ref[...], mg_ref[...]).astype(BF16)
    y = _dot(a_ref[...], wo_ref[:GMLP_WIDTH, :]) + _dot(m_n, wo_ref[GMLP_WIDTH:, :])
    o_ref[...] = h_ref[...] + g_ref[...] * y


def _outproj_call(h, a, m, mod_rows, mla_gain, wo, *, seq, tm=512):
    t, d = h.shape
    blocks_per_batch = seq // tm
    return pl.pallas_call(
        _outproj_kernel,
        grid=(t // tm,),
        in_specs=[
            pl.BlockSpec((tm, d), lambda i: (i, 0)),
            pl.BlockSpec((tm, GMLP_WIDTH), lambda i: (i, 0)),
            pl.BlockSpec((tm, MLA_WIDTH), lambda i: (i, 0)),
            pl.BlockSpec((None, 1, d), lambda i: ((i // blocks_per_batch) * N_MOD + 5, 0, 0)),
            _const_spec(mla_gain.shape), _const_spec(wo.shape),
        ],
        out_specs=pl.BlockSpec((tm, d), lambda i: (i, 0)),
        out_shape=jax.ShapeDtypeStruct((t, d), F32),
        compiler_params=pltpu.CompilerParams(
            dimension_semantics=("parallel",), vmem_limit_bytes=VMEM_LIMIT),
        name="mixer_out",
    )(h, a, m, mod_rows, mla_gain, wo)


def _ffn_weights(w_gate, w_up, w_down):
    return w_gate.astype(BF16), w_up.astype(BF16), w_down.astype(BF16)


def _half_swap(x):
    half = x.shape[-1] // 2
    return jnp.concatenate([x[..., half:], x[..., :half]], axis=-1)


def kernel(x, c, positions, w_ada, b_ada, ffn1_norm, ffn1_w_gate, ffn1_w_up, ffn1_w_down, mix_norm, w_in, gmlp_v_norm, gmlp_w_s, gmlp_b_s, q_lat_norm, w_uq, kv_lat_norm, w_ukv, q_nope_norm, q_rope_norm, k_nope_norm, k_rope_norm, out_norm_gmlp, out_norm_mla, w_out, ffn2_norm, ffn2_w_gate, ffn2_w_up, ffn2_w_down, final_norm):
    batch, seq, d = x.shape
    depth = w_ada.shape[0]
    t = batch * seq

    inv_freq = ROPE_THETA ** (-jnp.arange(0, QK_ROPE, 2, dtype=F32) / QK_ROPE)
    freq = jnp.tile(inv_freq, 4)[None, :]
    pos = positions.reshape(t, 1)
    c_pad = jnp.pad(c, ((0, 8 - batch), (0, 0)))

    def row(v):
        return v[None, :]

    h = x.reshape(t, d)
    for l in range(depth):
        mod = _mod_call(c_pad, w_ada[l], row(b_ada[l]))
        mod_rows = mod[:batch].reshape(batch * N_MOD, 1, d)

        w_in_t = w_in[l].T
        ni, nf = t // FFN_TM, D_FF // FFN_TF
        jobs = [
            (ffn2_w_gate[l], (d // ni, FFN_TF), lambda i, j: (i, j), True),
            (ffn2_w_up[l], (d // ni, FFN_TF), lambda i, j: (i, j), True),
            (ffn2_w_down[l], (FFN_TF, d // ni), lambda i, j: (j, i), True),
            (w_in_t, (IN_COLS, d // ni), lambda i, j: (0, i), False),
            (w_out[l], (d // ni, d), lambda i, j: (i, 0), False),
        ]
        h, (wg2, wu2, wd2, win, wout) = _ffn_call(
            h, mod_rows, 0, row(ffn1_norm[l]), row(final_norm[l]),
            *_ffn_weights(ffn1_w_gate[l], ffn1_w_up[l], ffn1_w_down[l]),
            seq=seq, final_norm=False, cast_jobs=jobs, tm=FFN_TM, tf=FFN_TF)

        k_rope_w = w_in_t[IN_COLS - QK_ROPE:]
        half = QK_ROPE // 2
        wkpe = jnp.concatenate([k_rope_w, k_rope_w[half:], k_rope_w[:half]], axis=0).astype(BF16)
        uq = w_uq[l].reshape(Q_RANK, MLA_HEADS, QK_NOPE + QK_ROPE)
        uq_rope = uq[..., QK_NOPE:]
        wuq = jnp.concatenate([uq[..., :QK_NOPE], uq_rope, _half_swap(uq_rope)], axis=-1)
        wuq = wuq.reshape(Q_RANK, MLA_HEADS * QK_PAD).astype(BF16)
        bs_full = jnp.repeat(gmlp_b_s[l].T, GMLP_GROUP, axis=1)
        ukv = w_ukv[l].reshape(KV_RANK, MLA_HEADS, QK_NOPE + V_HEAD)
        wk = ukv[..., :QK_NOPE].reshape(KV_RANK, MLA_HEADS * QK_NOPE).astype(BF16)
        wvt = ukv[..., QK_NOPE:].reshape(KV_RANK, MLA_WIDTH).T.astype(BF16)
        a, q, k, vt = _mix_call(
            h, mod_rows, row(mix_norm[l]), win, wkpe, row(gmlp_v_norm[l]), gmlp_w_s[l].astype(BF16), bs_full,
            row(q_lat_norm[l]), wuq, row(kv_lat_norm[l]), wk, wvt,
            row(q_nope_norm[l]), row(jnp.concatenate([q_rope_norm[l], _half_swap(q_rope_norm[l])])),
            row(k_nope_norm[l]), row(jnp.concatenate([k_rope_norm[l], _half_swap(k_rope_norm[l])])),
            row(out_norm_gmlp[l]), pos, freq, seq=seq)

        m = _attn_call(q, k, vt, batch=batch, seq=seq)
        h = _outproj_call(h, a, m, mod_rows, row(out_norm_mla[l]), wout, seq=seq)

        h, _ = _ffn_call(h, mod_rows, 6, row(ffn2_norm[l]), row(final_norm[l]), wg2, wu2, wd2,
                         seq=seq, final_norm=True, tm=FFN_TM, tf=FFN_TF)
    return h.reshape(batch, seq, d)
```

```python
import functools

import jax
import jax.numpy as jnp
from jax import lax
from jax.experimental import pallas as pl
from jax.experimental.pallas import tpu as pltpu

D_MODEL = 2048
GMLP_WIDTH = 1024
GMLP_GROUP = 128
GMLP_HEADS = GMLP_WIDTH // GMLP_GROUP
CHUNK = 128
V_HEAD = 128
MLA_WIDTH = 1024
MLA_HEADS = MLA_WIDTH // V_HEAD
QK_NOPE = 128
QK_ROPE = 64
Q_RANK = D_MODEL // 4
KV_RANK = D_MODEL // 8
IN_COLS = 2 * GMLP_WIDTH + Q_RANK + KV_RANK + QK_ROPE
D_FF = 5632
ROPE_THETA = 10000.0
EPS = 1e-6
N_MOD = 9
LOG2_E = 1.4426950408889634

LANES = 128
QK_PAD = 2 * LANES
IN_COLS_EXT = IN_COLS + QK_ROPE
VMEM_LIMIT = 56 * 1024 * 1024
BF16_SUBLANES = 16
ATTN_SUB = 512
FFN_TM = 512
FFN_TF = 512
FFN_LEAD_TF = 256
FFN_EDGE_SPLIT = 2

F32 = jnp.float32
BF16 = jnp.bfloat16


def _dot(a, b):
    return jnp.dot(a, b, preferred_element_type=F32)


def _dot_nt(a, b):
    return lax.dot_general(a, b, (((1,), (1,)), ((), ())), preferred_element_type=F32)


def _rms_rows(x, gain):
    return x * lax.rsqrt(jnp.mean(x * x, axis=-1, keepdims=True) + EPS) * gain


def _const_spec(shape):
    zeros = (0,) * len(shape)
    return pl.BlockSpec(shape, lambda *_: zeros, pipeline_mode=pl.Buffered(1))


def _mod_kernel(c_ref, w_ref, b_ref, o_ref):
    c = c_ref[...]
    c_act = (c * jax.nn.sigmoid(c)).astype(BF16)
    o_ref[...] = _dot(c_act, w_ref[...].astype(BF16)) + b_ref[...]


def _mod_call(c_pad, w_ada, b_ada, *, tn=1024):
    rows, d = c_pad.shape
    n = w_ada.shape[1]
    return pl.pallas_call(
        _mod_kernel,
        grid=(n // tn,),
        in_specs=[
            pl.BlockSpec((rows, d), lambda j: (0, 0)),
            pl.BlockSpec((d, tn), lambda j: (0, j)),
            pl.BlockSpec((1, tn), lambda j: (0, j)),
        ],
        out_specs=pl.BlockSpec((rows, tn), lambda j: (0, j)),
        out_shape=jax.ShapeDtypeStruct((rows, n), F32),
        compiler_params=pltpu.CompilerParams(
            dimension_semantics=("arbitrary",), vmem_limit_bytes=VMEM_LIMIT),
        name="adaln_mod",
    )(c_pad, w_ada, b_ada)


def _ffn_kernel(*refs, final_norm, cast_every_step, lead, seeded):
    n_cast = len(cast_every_step)
    x_ref, sh_ref, sc_ref, g_ref, nw_ref, fin_ref, wg_ref, wu_ref, wd_ref = refs[:9]
    pos = 9
    seed_ref = refs[pos] if seeded else None
    pos += int(seeded)
    cast_in = refs[pos:pos + n_cast]
    pos += n_cast
    o_ref = refs[pos]
    pos += 1
    weight_out = refs[pos:pos + 3] if lead else None
    pos += 3 * int(lead)
    cast_out = refs[pos:pos + n_cast]
    n_scr, acc_scr = refs[pos + n_cast:]
    i = pl.program_id(0)
    j = pl.program_id(1)
    last_j = pl.num_programs(1) - 1

    def cast(every_step):
        for src, dst, flag in zip(cast_in, cast_out, cast_every_step):
            if flag == every_step:
                dst[...] = src[...].astype(BF16)

    cast(True)

    @pl.when(j == 0)
    def _():
        cast(False)

    def step(first, last, n_split):
        if lead:
            wg, wu, wd = (w[...].astype(BF16) for w in (wg_ref, wu_ref, wd_ref))
            for dst, w in zip(weight_out, (wg, wu, wd)):
                dst[...] = w
        else:
            wg, wu, wd = wg_ref[...], wu_ref[...], wd_ref[...]
        sub = x_ref.shape[0] // n_split
        for r in range(n_split):
            rows = slice(r * sub, (r + 1) * sub)
            if first:
                gain = nw_ref[...] * (1.0 + sc_ref[...])
                n = (_rms_rows(x_ref[rows, :], gain) + sh_ref[...]).astype(BF16)
                n_scr[rows, :] = n
            else:
                n = n_scr[rows, :]
            hg = _dot(n, wg)
            hu = _dot(n, wu)
            hid = (hg * jax.nn.sigmoid(hg) * hu).astype(BF16)
            y = _dot(hid, wd)
            if not first:
                y = acc_scr[rows, :] + y
            if last:
                h = x_ref[rows, :] + (0.5 * g_ref[...]) * y
                if final_norm:
                    h = _rms_rows(h, fin_ref[...])
                o_ref[rows, :] = h
            else:
                acc_scr[rows, :] = y

    def when(cond):
        return pl.when(jnp.logical_and(cond, i > 0) if seeded else cond)

    @when(j == 0)
    def _():
        step(True, False, FFN_EDGE_SPLIT)

    @when(jnp.logical_and(j > 0, j < last_j))
    def _():
        step(False, False, 1)

    @when(j == last_j)
    def _():
        step(False, True, FFN_EDGE_SPLIT)

    if seeded:
        @pl.when(jnp.logical_and(j == last_j, i == 0))
        def _():
            o_ref[...] = seed_ref[...]


def _ffn_call(x, mod_rows, mod_base, norm_w, fin_w, wg, wu, wd, *, seq, final_norm, tm, tf,
              cast_jobs=(), lead=False, seed=None):
    t, d = x.shape
    nf = wg.shape[1] // tf
    blocks_per_batch = seq // tm
    seeded = seed is not None

    def row(k):
        return pl.BlockSpec((None, 1, d), lambda i, j: ((i // blocks_per_batch) * N_MOD + mod_base + k, 0, 0))

    def wj(i, j):
        return jnp.where(i == 0, 0, j) if seeded else j

    cast_arrays = [job[0] for job in cast_jobs]
    cast_specs = [pl.BlockSpec(job[1], job[2]) for job in cast_jobs]
    weight_specs = [
        pl.BlockSpec((d, tf), lambda i, j: (0, wj(i, j))),
        pl.BlockSpec((d, tf), lambda i, j: (0, wj(i, j))),
        pl.BlockSpec((tf, d), lambda i, j: (wj(i, j), 0)),
    ]
    n_rows = tm if lead else t
    outs = pl.pallas_call(
        functools.partial(_ffn_kernel, final_norm=final_norm, lead=lead, seeded=seeded,
                          cast_every_step=tuple(job[3] for job in cast_jobs)),
        grid=(n_rows // tm, nf),
        in_specs=[
            pl.BlockSpec((tm, d), lambda i, j: (i, 0)),
            row(0), row(1), row(2),
            pl.BlockSpec((1, d), lambda i, j: (0, 0)),
            pl.BlockSpec((1, d), lambda i, j: (0, 0)),
            *weight_specs,
            *([pl.BlockSpec((tm, d), lambda i, j: (0, 0))] if seeded else []),
            *cast_specs,
        ],
        out_specs=[pl.BlockSpec((tm, d), lambda i, j: (i, 0)),
                   *(weight_specs if lead else []), *cast_specs],
        out_shape=[jax.ShapeDtypeStruct((n_rows, d), F32),
                   *([jax.ShapeDtypeStruct(w.shape, BF16) for w in (wg, wu, wd)] if lead else []),
                   *(jax.ShapeDtypeStruct(a.shape, BF16) for a in cast_arrays)],
        scratch_shapes=[pltpu.VMEM((tm, d), BF16), pltpu.VMEM((tm, d), F32)],
        compiler_params=pltpu.CompilerParams(
            dimension_semantics=("parallel", "arbitrary"), vmem_limit_bytes=VMEM_LIMIT),
        name="swiglu_ffn_lead" if lead else "swiglu_ffn",
    )(x, mod_rows, mod_rows, mod_rows, norm_w, fin_w, wg, wu, wd,
      *([seed] if seeded else []), *cast_arrays)
    return outs[0], outs[1:]


def _mix_kernel(h_ref, sh_ref, sc_ref, nw_ref, win_ref, wkpe_ref, vn_ref, ws_ref, bs_ref,
                qln_ref, wuq_ref, kvn_ref, wk_ref, wvt_ref, qng_ref, qrg_ref, kng_ref, krg_ref,
                ong_ref, pos_ref, freq_ref,
                a_ref, q_ref, k_ref, vt_ref,
                u_scr, v_scr, a_scr, *, tm):
    n = (_rms_rows(h_ref[...], nw_ref[...] * (1.0 + sc_ref[...])) + sh_ref[...]).astype(BF16)

    o1 = GMLP_WIDTH
    o2 = 2 * GMLP_WIDTH
    o3 = o2 + Q_RANK
    o4 = o3 + KV_RANK
    u_scr[...] = jax.nn.gelu(_dot_nt(n, win_ref[:o1, :]))
    vv = jax.nn.gelu(_dot_nt(n, win_ref[o1:o2, :]))
    v_scr[...] = _rms_rows(vv, vn_ref[...]).astype(BF16)
    for c in range(tm // CHUNK):
        rows = slice(c * CHUNK, (c + 1) * CHUNK)
        for g in range(GMLP_HEADS):
            cols = slice(g * GMLP_GROUP, (g + 1) * GMLP_GROUP)
            mixed = _dot(ws_ref[g], v_scr[rows, cols]) + bs_ref[:, cols]
            a_scr[rows, cols] = u_scr[rows, cols] * mixed
    a_ref[...] = _rms_rows(a_scr[...], ong_ref[...]).astype(BF16)

    ang = pos_ref[...].astype(F32) * freq_ref[...]
    lane = lax.broadcasted_iota(jnp.int32, ang.shape, 1)
    sin = jnp.sin(ang)
    rot = jnp.where(lane < QK_ROPE, jnp.cos(ang), jnp.where(lane < QK_ROPE + QK_ROPE // 2, -sin, sin))

    def rope(z, gain):
        w = _rms_rows(z, gain) * rot
        return w + pltpu.roll(w, QK_ROPE, axis=1)

    scale = (QK_NOPE + QK_ROPE) ** -0.5 * LOG2_E
    q_lat = _rms_rows(_dot_nt(n, win_ref[o2:o3, :]), qln_ref[...]).astype(BF16)
    for hd in range(MLA_HEADS):
        base = hd * QK_PAD
        qq = _dot(q_lat, wuq_ref[:, base:base + QK_PAD])
        q_ref[:, base:base + LANES] = (_rms_rows(qq[:, :LANES], qng_ref[...]) * scale).astype(BF16)
        q_ref[:, base + LANES:base + QK_PAD] = (rope(qq[:, LANES:], qrg_ref[...]) * scale).astype(BF16)

    k_pe = rope(_dot_nt(n, wkpe_ref[...]), krg_ref[...])
    k_pe = jnp.where(lane < QK_ROPE, k_pe, 0.0).astype(BF16)
    kv_lat = _rms_rows(_dot_nt(n, win_ref[o3:o4, :]), kvn_ref[...]).astype(BF16)
    k_nope = _dot(kv_lat, wk_ref[...])
    for hd in range(MLA_HEADS):
        base = hd * QK_PAD
        cols = slice(hd * QK_NOPE, (hd + 1) * QK_NOPE)
        k_ref[:, base:base + LANES] = _rms_rows(k_nope[:, cols], kng_ref[...]).astype(BF16)
        k_ref[:, base + LANES:base + QK_PAD] = k_pe
    vt_ref[...] = _dot_nt(wvt_ref[...], kv_lat).astype(BF16)


def _mix_call(h, mod_rows, mix_norm, win, wkpe, v_norm, ws, bs_full, q_lat_norm, wuq, kv_lat_norm, wk, wvt,
              qn_g, qr_g, kn_g, kr_g, on_g, pos, freq, *, seq, tm=512):
    t, d = h.shape
    blocks_per_batch = seq // tm
    batch = t // seq

    def row(k):
        return pl.BlockSpec((None, 1, d), lambda i: ((i // blocks_per_batch) * N_MOD + 3 + k, 0, 0))

    def tok(width):
        return pl.BlockSpec((tm, width), lambda i: (i, 0))

    return pl.pallas_call(
        functools.partial(_mix_kernel, tm=tm),
        grid=(t // tm,),
        in_specs=[
            tok(d), row(0), row(1),
            _const_spec(mix_norm.shape), _const_spec(win.shape), _const_spec(wkpe.shape),
            _const_spec(v_norm.shape),
            _const_spec(ws.shape), _const_spec(bs_full.shape),
            _const_spec(q_lat_norm.shape), _const_spec(wuq.shape),
            _const_spec(kv_lat_norm.shape), _const_spec(wk.shape), _const_spec(wvt.shape),
            _const_spec(qn_g.shape), _const_spec(qr_g.shape), _const_spec(kn_g.shape),
            _const_spec(kr_g.shape), _const_spec(on_g.shape),
            tok(1), _const_spec(freq.shape),
        ],
        out_specs=[
            tok(GMLP_WIDTH), tok(MLA_HEADS * QK_PAD), tok(MLA_HEADS * QK_PAD),
            pl.BlockSpec((MLA_WIDTH, tm), lambda i: (i // blocks_per_batch, i % blocks_per_batch)),
        ],
        out_shape=[
            jax.ShapeDtypeStruct((t, GMLP_WIDTH), BF16),
            jax.ShapeDtypeStruct((t, MLA_HEADS * QK_PAD), BF16),
            jax.ShapeDtypeStruct((t, MLA_HEADS * QK_PAD), BF16),
            jax.ShapeDtypeStruct((batch * MLA_WIDTH, seq), BF16),
        ],
        scratch_shapes=[
            pltpu.VMEM((tm, GMLP_WIDTH), F32),
            pltpu.VMEM((tm, GMLP_WIDTH), BF16),
            pltpu.VMEM((tm, GMLP_WIDTH), F32),
        ],
        compiler_params=pltpu.CompilerParams(
            dimension_semantics=("parallel",), vmem_limit_bytes=VMEM_LIMIT),
        name="mixer_in",
    )(h, mod_rows, mod_rows, mix_norm, win, wkpe, v_norm, ws, bs_full, q_lat_norm, wuq, kv_lat_norm, wk, wvt,
      qn_g, qr_g, kn_g, kr_g, on_g, pos, freq)


def _attn_kernel(q_ref, k_ref, vt_ref, o_ref, vx_scr, s_scr, p_scr, *, sub, nsub):
    vx_scr[:V_HEAD, :] = vt_ref[...]
    vx_scr[V_HEAD:, :] = jnp.ones((vx_scr.shape[0] - V_HEAD, vx_scr.shape[1]), BF16)

    def rows(i):
        return slice(i * sub, (i + 1) * sub)

    def qk(i, slot):
        s_scr[slot] = _dot_nt(k_ref[...], q_ref[rows(i), :])

    def softmax(slot):
        s = s_scr[slot]
        p_scr[slot] = jnp.exp2(s - jnp.max(s, axis=0, keepdims=True)).astype(BF16)

    def pv(i, slot):
        ox = _dot(vx_scr[...], p_scr[slot])
        o_ref[rows(i), :] = (ox[:V_HEAD, :] / ox[V_HEAD:V_HEAD + 1, :]).T

    qk(0, 0)
    softmax(0)
    qk(1, 1)
    for i in range(nsub - 2):
        qk(i + 2, i % 2)
        softmax(1 - i % 2)
        pv(i, i % 2)
    softmax(1 - nsub % 2)
    pv(nsub - 2, nsub % 2)
    pv(nsub - 1, 1 - nsub % 2)


def _attn_call(q, k, vt, *, batch, seq, sub=ATTN_SUB):
    t = q.shape[0]
    nsub = seq // sub
    assert nsub >= 3
    return pl.pallas_call(
        functools.partial(_attn_kernel, sub=sub, nsub=nsub),
        grid=(batch, MLA_HEADS),
        in_specs=[
            pl.BlockSpec((seq, QK_PAD), lambda b, h: (b, h)),
            pl.BlockSpec((seq, QK_PAD), lambda b, h: (b, h)),
            pl.BlockSpec((V_HEAD, seq), lambda b, h: (b * MLA_HEADS + h, 0)),
        ],
        out_specs=pl.BlockSpec((seq, V_HEAD), lambda b, h: (b, h)),
        out_shape=jax.ShapeDtypeStruct((t, MLA_WIDTH), F32),
        scratch_shapes=[
            pltpu.VMEM((V_HEAD + BF16_SUBLANES, seq), BF16),
            pltpu.VMEM((2, seq, sub), F32),
            pltpu.VMEM((2, seq, sub), BF16),
        ],
        compiler_params=pltpu.CompilerParams(
            dimension_semantics=("parallel", "parallel"), vmem_limit_bytes=VMEM_LIMIT),
        name="mla_attention",
    )(q, k, vt)


def _outproj_kernel(h_ref, a_ref, m_ref, g_ref, mg_ref, wo_ref, o_ref):
    m_n = _rms_rows(m_ref[...], mg_ref[...]).astype(BF16)
    y = _dot(a_ref[...], wo_ref[:GMLP_WIDTH, :]) + _dot(m_n, wo_ref[GMLP_WIDTH:, :])
    o_ref[...] = h_ref[...] + g_ref[...] * y


def _outproj_call(h, a, m, mod_rows, mla_gain, wo, *, seq, tm=512):
    t, d = h.shape
    blocks_per_batch = seq // tm
    return pl.pallas_call(
        _outproj_kernel,
        grid=(t // tm,),
        in_specs=[
            pl.BlockSpec((tm, d), lambda i: (i, 0)),
            pl.BlockSpec((tm, GMLP_WIDTH), lambda i: (i, 0)),
            pl.BlockSpec((tm, MLA_WIDTH), lambda i: (i, 0)),
            pl.BlockSpec((None, 1, d), lambda i: ((i // blocks_per_batch) * N_MOD + 5, 0, 0)),
            _const_spec(mla_gain.shape), _const_spec(wo.shape),
        ],
        out_specs=pl.BlockSpec((tm, d), lambda i: (i, 0)),
        out_shape=jax.ShapeDtypeStruct((t, d), F32),
        compiler_params=pltpu.CompilerParams(
            dimension_semantics=("parallel",), vmem_limit_bytes=VMEM_LIMIT),
        name="mixer_out",
    )(h, a, m, mod_rows, mla_gain, wo)


def _half_swap(x):
    half = x.shape[-1] // 2
    return jnp.concatenate([x[..., half:], x[..., :half]], axis=-1)


def kernel(x, c, positions, w_ada, b_ada, ffn1_norm, ffn1_w_gate, ffn1_w_up, ffn1_w_down, mix_norm, w_in, gmlp_v_norm, gmlp_w_s, gmlp_b_s, q_lat_norm, w_uq, kv_lat_norm, w_ukv, q_nope_norm, q_rope_norm, k_nope_norm, k_rope_norm, out_norm_gmlp, out_norm_mla, w_out, ffn2_norm, ffn2_w_gate, ffn2_w_up, ffn2_w_down, final_norm):
    batch, seq, d = x.shape
    depth = w_ada.shape[0]
    t = batch * seq

    inv_freq = ROPE_THETA ** (-jnp.arange(0, QK_ROPE, 2, dtype=F32) / QK_ROPE)
    freq = jnp.tile(inv_freq, 4)[None, :]
    pos = positions.reshape(t, 1)
    c_pad = jnp.pad(c, ((0, 8 - batch), (0, 0)))

    def row(v):
        return v[None, :]

    h = x.reshape(t, d)
    for l in range(depth):
        mod = _mod_call(c_pad, w_ada[l], row(b_ada[l]))
        mod_rows = mod[:batch].reshape(batch * N_MOD, 1, d)

        w_in_t = w_in[l].T
        ni, nf = t // FFN_TM, D_FF // FFN_TF
        jobs = [
            (ffn2_w_gate[l], (d // ni, FFN_TF), lambda i, j: (i, j), True),
            (ffn2_w_up[l], (d // ni, FFN_TF), lambda i, j: (i, j), True),
            (ffn2_w_down[l], (FFN_TF, d // ni), lambda i, j: (j, i), True),
            (w_in_t, (IN_COLS, d // ni), lambda i, j: (0, i), False),
            (w_out[l], (d // ni, d), lambda i, j: (i, 0), False),
        ]
        h_lead, ffn1_weights = _ffn_call(
            h, mod_rows, 0, row(ffn1_norm[l]), row(final_norm[l]),
            ffn1_w_gate[l], ffn1_w_up[l], ffn1_w_down[l],
            seq=seq, final_norm=False, tm=FFN_TM, tf=FFN_LEAD_TF, lead=True)
        h, (wg2, wu2, wd2, win, wout) = _ffn_call(
            h, mod_rows, 0, row(ffn1_norm[l]), row(final_norm[l]), *ffn1_weights,
            seq=seq, final_norm=False, cast_jobs=jobs, seed=h_lead, tm=FFN_TM, tf=FFN_TF)

        k_rope_w = w_in_t[IN_COLS - QK_ROPE:]
        half = QK_ROPE // 2
        wkpe = jnp.concatenate([k_rope_w, k_rope_w[half:], k_rope_w[:half]], axis=0).astype(BF16)
        uq = w_uq[l].reshape(Q_RANK, MLA_HEADS, QK_NOPE + QK_ROPE)
        uq_rope = uq[..., QK_NOPE:]
        wuq = jnp.concatenate([uq[..., :QK_NOPE], uq_rope, _half_swap(uq_rope)], axis=-1)
        wuq = wuq.reshape(Q_RANK, MLA_HEADS * QK_PAD).astype(BF16)
        bs_full = jnp.repeat(gmlp_b_s[l].T, GMLP_GROUP, axis=1)
        ukv = w_ukv[l].reshape(KV_RANK, MLA_HEADS, QK_NOPE + V_HEAD)
        wk = ukv[..., :QK_NOPE].reshape(KV_RANK, MLA_HEADS * QK_NOPE).astype(BF16)
        wvt = ukv[..., QK_NOPE:].reshape(KV_RANK, MLA_WIDTH).T.astype(BF16)
        a, q, k, vt = _mix_call(
            h, mod_rows, row(mix_norm[l]), win, wkpe, row(gmlp_v_norm[l]), gmlp_w_s[l].astype(BF16), bs_full,
            row(q_lat_norm[l]), wuq, row(kv_lat_norm[l]), wk, wvt,
            row(q_nope_norm[l]), row(jnp.concatenate([q_rope_norm[l], _half_swap(q_rope_norm[l])])),
            row(k_nope_norm[l]), row(jnp.concatenate([k_rope_norm[l], _half_swap(k_rope_norm[l])])),
            row(out_norm_gmlp[l]), pos, freq, seq=seq)

        m = _attn_call(q, k, vt, batch=batch, seq=seq)
        h = _outproj_call(h, a, m, mod_rows, row(out_norm_mla[l]), wout, seq=seq)

        h, _ = _ffn_call(h, mod_rows, 6, row(ffn2_norm[l]), row(final_norm[l]), wg2, wu2, wd2,
                         seq=seq, final_norm=True, tm=FFN_TM, tf=FFN_TF)
    return h.reshape(batch, seq, d)
```

```python
import functools

import jax
import jax.numpy as jnp
from jax import lax
from jax.experimental import pallas as pl
from jax.experimental.pallas import tpu as pltpu

D_MODEL = 2048
GMLP_WIDTH = 1024
GMLP_GROUP = 128
GMLP_HEADS = GMLP_WIDTH // GMLP_GROUP
CHUNK = 128
V_HEAD = 128
MLA_WIDTH = 1024
MLA_HEADS = MLA_WIDTH // V_HEAD
QK_NOPE = 128
QK_ROPE = 64
Q_RANK = D_MODEL // 4
KV_RANK = D_MODEL // 8
IN_COLS = 2 * GMLP_WIDTH + Q_RANK + KV_RANK + QK_ROPE
D_FF = 5632
ROPE_THETA = 10000.0
EPS = 1e-6
N_MOD = 9
LOG2_E = 1.4426950408889634

LANES = 128
QK_PAD = 2 * LANES
IN_COLS_EXT = IN_COLS + QK_ROPE
VMEM_LIMIT = 56 * 1024 * 1024
BF16_SUBLANES = 16
ATTN_SLOTS = 2
ATTN_SUB = 512
FFN_TM = 512
FFN_TF = 512
MIX_SPLIT = 1
FFN_LEAD_TF = 256
FFN_K_SPLIT = 2
FFN_EDGE_SPLIT = 2

F32 = jnp.float32
BF16 = jnp.bfloat16


def _dot(a, b):
    return jnp.dot(a, b, preferred_element_type=F32)


def _dot_nt(a, b):
    return lax.dot_general(a, b, (((1,), (1,)), ((), ())), preferred_element_type=F32)


def _rms_rows(x, gain):
    return x * lax.rsqrt(jnp.mean(x * x, axis=-1, keepdims=True) + EPS) * gain


def _const_spec(shape):
    zeros = (0,) * len(shape)
    return pl.BlockSpec(shape, lambda *_: zeros, pipeline_mode=pl.Buffered(1))


def _mod_kernel(c_ref, w_ref, b_ref, o_ref):
    c = c_ref[...]
    c_act = (c * jax.nn.sigmoid(c)).astype(BF16)
    o_ref[...] = _dot(c_act, w_ref[...].astype(BF16)) + b_ref[...]


def _mod_call(c_pad, w_ada, b_ada, *, tn=1024):
    rows, d = c_pad.shape
    n = w_ada.shape[1]
    return pl.pallas_call(
        _mod_kernel,
        grid=(n // tn,),
        in_specs=[
            pl.BlockSpec((rows, d), lambda j: (0, 0)),
            pl.BlockSpec((d, tn), lambda j: (0, j)),
            pl.BlockSpec((1, tn), lambda j: (0, j)),
        ],
        out_specs=pl.BlockSpec((rows, tn), lambda j: (0, j)),
        out_shape=jax.ShapeDtypeStruct((rows, n), F32),
        compiler_params=pltpu.CompilerParams(
            dimension_semantics=("arbitrary",), vmem_limit_bytes=VMEM_LIMIT),
        name="adaln_mod",
    )(c_pad, w_ada, b_ada)


def _ffn_kernel(*refs, final_norm, cast_every_step, lead, seeded):
    n_cast = len(cast_every_step)
    x_ref, sh_ref, sc_ref, g_ref, nw_ref, fin_ref, wg_ref, wu_ref, wd_ref = refs[:9]
    pos = 9
    seed_ref = refs[pos] if seeded else None
    pos += int(seeded)
    cast_in = refs[pos:pos + n_cast]
    pos += n_cast
    o_ref = refs[pos]
    pos += 1
    weight_out = refs[pos:pos + 3] if lead else None
    pos += 3 * int(lead)
    cast_out = refs[pos:pos + n_cast]
    n_scr, acc_scr = refs[pos + n_cast:]
    i = pl.program_id(0)
    j = pl.program_id(1)
    last_j = pl.num_programs(1) - 1

    def cast(every_step):
        for src, dst, flag in zip(cast_in, cast_out, cast_every_step):
            if flag == every_step:
                dst[...] = src[...].astype(BF16)

    cast(True)

    @pl.when(j == 0)
    def _():
        cast(False)

    def step(first, last, n_split):
        if lead:
            wg, wu, wd = (w[...].astype(BF16) for w in (wg_ref, wu_ref, wd_ref))
            for dst, w in zip(weight_out, (wg, wu, wd)):
                dst[...] = w
        else:
            wg, wu, wd = wg_ref[...], wu_ref[...], wd_ref[...]
        sub = x_ref.shape[0] // n_split
        for r in range(n_split):
            rows = slice(r * sub, (r + 1) * sub)
            if first:
                gain = nw_ref[...] * (1.0 + sc_ref[...])
                n = (_rms_rows(x_ref[rows, :], gain) + sh_ref[...]).astype(BF16)
                n_scr[rows, :] = n
            else:
                n = n_scr[rows, :]
            y = None
            k_split = FFN_K_SPLIT if n_split == 1 else 1
            half = wg.shape[1] // k_split
            for c in range(k_split):
                cols = slice(c * half, (c + 1) * half)
                hg = _dot(n, wg[:, cols])
                hu = _dot(n, wu[:, cols])
                hid = (hg * jax.nn.sigmoid(hg) * hu).astype(BF16)
                part = _dot(hid, wd[cols, :])
                y = part if y is None else y + part
            if not first:
                y = acc_scr[rows, :] + y
            if last:
                h = x_ref[rows, :] + (0.5 * g_ref[...]) * y
                if final_norm:
                    h = _rms_rows(h, fin_ref[...])
                o_ref[rows, :] = h
            else:
                acc_scr[rows, :] = y

    def when(cond):
        return pl.when(jnp.logical_and(cond, i > 0) if seeded else cond)

    @when(j == 0)
    def _():
        step(True, False, FFN_EDGE_SPLIT)

    @when(jnp.logical_and(j > 0, j < last_j))
    def _():
        step(False, False, 1)

    @when(j == last_j)
    def _():
        step(False, True, FFN_EDGE_SPLIT)

    if seeded:
        @pl.when(jnp.logical_and(j == last_j, i == 0))
        def _():
            o_ref[...] = seed_ref[...]


def _ffn_call(x, mod_rows, mod_base, norm_w, fin_w, wg, wu, wd, *, seq, final_norm, tm, tf,
              cast_jobs=(), lead=False, seed=None):
    t, d = x.shape
    nf = wg.shape[1] // tf
    blocks_per_batch = seq // tm
    seeded = seed is not None

    def row(k):
        return pl.BlockSpec((None, 1, d), lambda i, j: ((i // blocks_per_batch) * N_MOD + mod_base + k, 0, 0))

    def wj(i, j):
        return jnp.where(i == 0, 0, j) if seeded else j

    cast_arrays = [job[0] for job in cast_jobs]
    cast_specs = [pl.BlockSpec(job[1], job[2]) for job in cast_jobs]
    weight_specs = [
        pl.BlockSpec((d, tf), lambda i, j: (0, wj(i, j))),
        pl.BlockSpec((d, tf), lambda i, j: (0, wj(i, j))),
        pl.BlockSpec((tf, d), lambda i, j: (wj(i, j), 0)),
    ]
    n_rows = tm if lead else t
    outs = pl.pallas_call(
        functools.partial(_ffn_kernel, final_norm=final_norm, lead=lead, seeded=seeded,
                          cast_every_step=tuple(job[3] for job in cast_jobs)),
        grid=(n_rows // tm, nf),
        in_specs=[
            pl.BlockSpec((tm, d), lambda i, j: (i, 0)),
            row(0), row(1), row(2),
            pl.BlockSpec((1, d), lambda i, j: (0, 0)),
            pl.BlockSpec((1, d), lambda i, j: (0, 0)),
            *weight_specs,
            *([pl.BlockSpec((tm, d), lambda i, j: (0, 0))] if seeded else []),
            *cast_specs,
        ],
        out_specs=[pl.BlockSpec((tm, d), lambda i, j: (i, 0)),
                   *(weight_specs if lead else []), *cast_specs],
        out_shape=[jax.ShapeDtypeStruct((n_rows, d), F32),
                   *([jax.ShapeDtypeStruct(w.shape, BF16) for w in (wg, wu, wd)] if lead else []),
                   *(jax.ShapeDtypeStruct(a.shape, BF16) for a in cast_arrays)],
        scratch_shapes=[pltpu.VMEM((tm, d), BF16), pltpu.VMEM((tm, d), F32)],
        compiler_params=pltpu.CompilerParams(
            dimension_semantics=("parallel", "arbitrary"), vmem_limit_bytes=VMEM_LIMIT),
        name="swiglu_ffn_lead" if lead else "swiglu_ffn",
    )(x, mod_rows, mod_rows, mod_rows, norm_w, fin_w, wg, wu, wd,
      *([seed] if seeded else []), *cast_arrays)
    return outs[0], outs[1:]


def _mix_kernel(h_ref, sh_ref, sc_ref, nw_ref, win_ref, wkpe_ref, vn_ref, ws_ref, bs_ref,
                qln_ref, wuq_ref, kvn_ref, wk_ref, wvt_ref, qng_ref, qrg_ref, kng_ref, krg_ref,
                ong_ref, pos_ref, freq_ref,
                a_ref, q_ref, k_ref, vt_ref,
                u_scr, v_scr, a_scr, *, tm):
    o1 = GMLP_WIDTH
    o2 = 2 * GMLP_WIDTH
    o3 = o2 + Q_RANK
    o4 = o3 + KV_RANK
    scale = (QK_NOPE + QK_ROPE) ** -0.5 * LOG2_E
    sub = tm // MIX_SPLIT

    for r in range(MIX_SPLIT):
        tok = slice(r * sub, (r + 1) * sub)
        gain = nw_ref[...] * (1.0 + sc_ref[...])
        n = (_rms_rows(h_ref[tok, :], gain) + sh_ref[...]).astype(BF16)

        u_scr[tok, :] = jax.nn.gelu(_dot_nt(n, win_ref[:o1, :]))
        vv = jax.nn.gelu(_dot_nt(n, win_ref[o1:o2, :]))
        v_scr[tok, :] = _rms_rows(vv, vn_ref[...]).astype(BF16)
        for c in range(r * sub // CHUNK, (r + 1) * sub // CHUNK):
            rows = slice(c * CHUNK, (c + 1) * CHUNK)
            for g in range(GMLP_HEADS):
                cols = slice(g * GMLP_GROUP, (g + 1) * GMLP_GROUP)
                mixed = _dot(ws_ref[g], v_scr[rows, cols]) + bs_ref[:, cols]
                a_scr[rows, cols] = u_scr[rows, cols] * mixed
        a_ref[tok, :] = _rms_rows(a_scr[tok, :], ong_ref[...]).astype(BF16)

        ang = pos_ref[tok, :].astype(F32) * freq_ref[...]
        lane = lax.broadcasted_iota(jnp.int32, ang.shape, 1)
        sin = jnp.sin(ang)
        rot = jnp.where(lane < QK_ROPE, jnp.cos(ang), jnp.where(lane < QK_ROPE + QK_ROPE // 2, -sin, sin))

        def rope(z, gain):
            w = _rms_rows(z, gain) * rot
            return w + pltpu.roll(w, QK_ROPE, axis=1)

        q_lat = _rms_rows(_dot_nt(n, win_ref[o2:o3, :]), qln_ref[...]).astype(BF16)
        for hd in range(MLA_HEADS):
            base = hd * QK_PAD
            qq = _dot(q_lat, wuq_ref[:, base:base + QK_PAD])
            q_ref[tok, base:base + LANES] = (_rms_rows(qq[:, :LANES], qng_ref[...]) * scale).astype(BF16)
            q_ref[tok, base + LANES:base + QK_PAD] = (rope(qq[:, LANES:], qrg_ref[...]) * scale).astype(BF16)

        k_pe = rope(_dot_nt(n, wkpe_ref[...]), krg_ref[...])
        k_pe = jnp.where(lane < QK_ROPE, k_pe, 0.0).astype(BF16)
        kv_lat = _rms_rows(_dot_nt(n, win_ref[o3:o4, :]), kvn_ref[...]).astype(BF16)
        k_nope = _dot(kv_lat, wk_ref[...])
        for hd in range(MLA_HEADS):
            base = hd * QK_PAD
            cols = slice(hd * QK_NOPE, (hd + 1) * QK_NOPE)
            k_ref[tok, base:base + LANES] = _rms_rows(k_nope[:, cols], kng_ref[...]).astype(BF16)
            k_ref[tok, base + LANES:base + QK_PAD] = k_pe
        vt_ref[:, tok] = _dot_nt(wvt_ref[...], kv_lat).astype(BF16)


def _mix_call(h, mod_rows, mix_norm, win, wkpe, v_norm, ws, bs_full, q_lat_norm, wuq, kv_lat_norm, wk, wvt,
              qn_g, qr_g, kn_g, kr_g, on_g, pos, freq, *, seq, tm=512):
    t, d = h.shape
    blocks_per_batch = seq // tm
    batch = t // seq

    def row(k):
        return pl.BlockSpec((None, 1, d), lambda i: ((i // blocks_per_batch) * N_MOD + 3 + k, 0, 0))

    def tok(width):
        return pl.BlockSpec((tm, width), lambda i: (i, 0))

    return pl.pallas_call(
        functools.partial(_mix_kernel, tm=tm),
        grid=(t // tm,),
        in_specs=[
            tok(d), row(0), row(1),
            _const_spec(mix_norm.shape), _const_spec(win.shape), _const_spec(wkpe.shape),
            _const_spec(v_norm.shape),
            _const_spec(ws.shape), _const_spec(bs_full.shape),
            _const_spec(q_lat_norm.shape), _const_spec(wuq.shape),
            _const_spec(kv_lat_norm.shape), _const_spec(wk.shape), _const_spec(wvt.shape),
            _const_spec(qn_g.shape), _const_spec(qr_g.shape), _const_spec(kn_g.shape),
            _const_spec(kr_g.shape), _const_spec(on_g.shape),
            tok(1), _const_spec(freq.shape),
        ],
        out_specs=[
            tok(GMLP_WIDTH), tok(MLA_HEADS * QK_PAD), tok(MLA_HEADS * QK_PAD),
            pl.BlockSpec((MLA_WIDTH, tm), lambda i: (i // blocks_per_batch, i % blocks_per_batch)),
        ],
        out_shape=[
            jax.ShapeDtypeStruct((t, GMLP_WIDTH), BF16),
            jax.ShapeDtypeStruct((t, MLA_HEADS * QK_PAD), BF16),
            jax.ShapeDtypeStruct((t, MLA_HEADS * QK_PAD), BF16),
            jax.ShapeDtypeStruct((batch * MLA_WIDTH, seq), BF16),
        ],
        scratch_shapes=[
            pltpu.VMEM((tm, GMLP_WIDTH), F32),
            pltpu.VMEM((tm, GMLP_WIDTH), BF16),
            pltpu.VMEM((tm, GMLP_WIDTH), F32),
        ],
        compiler_params=pltpu.CompilerParams(
            dimension_semantics=("parallel",), vmem_limit_bytes=VMEM_LIMIT),
        name="mixer_in",
    )(h, mod_rows, mod_rows, mix_norm, win, wkpe, v_norm, ws, bs_full, q_lat_norm, wuq, kv_lat_norm, wk, wvt,
      qn_g, qr_g, kn_g, kr_g, on_g, pos, freq)


def _attn_kernel(q_ref, k_ref, vt_ref, o_ref, vx_scr, s_scr, p_scr, *, sub, nsub):
    vx_scr[:V_HEAD, :] = vt_ref[...]
    vx_scr[V_HEAD:, :] = jnp.ones((vx_scr.shape[0] - V_HEAD, vx_scr.shape[1]), BF16)

    def rows(i):
        return slice(i * sub, (i + 1) * sub)

    def qk(i):
        s_scr[i % ATTN_SLOTS] = _dot_nt(k_ref[...], q_ref[rows(i), :])

    def softmax(i):
        s = s_scr[i % ATTN_SLOTS]
        p_scr[i % ATTN_SLOTS] = jnp.exp2(s - jnp.max(s, axis=0, keepdims=True)).astype(BF16)

    def pv(i):
        ox = _dot(vx_scr[...], p_scr[i % ATTN_SLOTS])
        o_ref[rows(i), :] = (ox[:V_HEAD, :] / ox[V_HEAD:V_HEAD + 1, :]).T

    qk(0)
    softmax(0)
    qk(1)
    for i in range(nsub - 2):
        qk(i + 2)
        softmax(i + 1)
        pv(i)
    softmax(nsub - 1)
    pv(nsub - 2)
    pv(nsub - 1)


def _attn_call(q, k, vt, *, batch, seq, sub=ATTN_SUB):
    t = q.shape[0]
    nsub = seq // sub
    assert nsub >= 3
    return pl.pallas_call(
        functools.partial(_attn_kernel, sub=sub, nsub=nsub),
        grid=(batch, MLA_HEADS),
        in_specs=[
            pl.BlockSpec((seq, QK_PAD), lambda b, h: (b, h)),
            pl.BlockSpec((seq, QK_PAD), lambda b, h: (b, h)),
            pl.BlockSpec((V_HEAD, seq), lambda b, h: (b * MLA_HEADS + h, 0)),
        ],
        out_specs=pl.BlockSpec((seq, V_HEAD), lambda b, h: (b, h)),
        out_shape=jax.ShapeDtypeStruct((t, MLA_WIDTH), F32),
        scratch_shapes=[
            pltpu.VMEM((V_HEAD + BF16_SUBLANES, seq), BF16),
            pltpu.VMEM((ATTN_SLOTS, seq, sub), F32),
            pltpu.VMEM((ATTN_SLOTS, seq, sub), BF16),
        ],
        compiler_params=pltpu.CompilerParams(
            dimension_semantics=("parallel", "parallel"), vmem_limit_bytes=VMEM_LIMIT),
        name="mla_attention",
    )(q, k, vt)


def _outproj_kernel(h_ref, a_ref, m_ref, g_ref, mg_ref, wo_ref, o_ref):
    m_n = _rms_rows(m_ref[...], mg_ref[...]).astype(BF16)
    y = _dot(a_ref[...], wo_ref[:GMLP_WIDTH, :]) + _dot(m_n, wo_ref[GMLP_WIDTH:, :])
    o_ref[...] = h_ref[...] + g_ref[...] * y


def _outproj_call(h, a, m, mod_rows, mla_gain, wo, *, seq, tm=512):
    t, d = h.shape
    blocks_per_batch = seq // tm
    return pl.pallas_call(
        _outproj_kernel,
        grid=(t // tm,),
        in_specs=[
            pl.BlockSpec((tm, d), lambda i: (i, 0)),
            pl.BlockSpec((tm, GMLP_WIDTH), lambda i: (i, 0)),
            pl.BlockSpec((tm, MLA_WIDTH), lambda i: (i, 0)),
            pl.BlockSpec((None, 1, d), lambda i: ((i // blocks_per_batch) * N_MOD + 5, 0, 0)),
            _const_spec(mla_gain.shape), _const_spec(wo.shape),
        ],
        out_specs=pl.BlockSpec((tm, d), lambda i: (i, 0)),
        out_shape=jax.ShapeDtypeStruct((t, d), F32),
        compiler_params=pltpu.CompilerParams(
            dimension_semantics=("parallel",), vmem_limit_bytes=VMEM_LIMIT),
        name="mixer_out",
    )(h, a, m, mod_rows, mla_gain, wo)


def _half_swap(x):
    half = x.shape[-1] // 2
    return jnp.concatenate([x[..., half:], x[..., :half]], axis=-1)


def kernel(x, c, positions, w_ada, b_ada, ffn1_norm, ffn1_w_gate, ffn1_w_up, ffn1_w_down, mix_norm, w_in, gmlp_v_norm, gmlp_w_s, gmlp_b_s, q_lat_norm, w_uq, kv_lat_norm, w_ukv, q_nope_norm, q_rope_norm, k_nope_norm, k_rope_norm, out_norm_gmlp, out_norm_mla, w_out, ffn2_norm, ffn2_w_gate, ffn2_w_up, ffn2_w_down, final_norm):
    batch, seq, d = x.shape
    depth = w_ada.shape[0]
    t = batch * seq

    inv_freq = ROPE_THETA ** (-jnp.arange(0, QK_ROPE, 2, dtype=F32) / QK_ROPE)
    freq = jnp.tile(inv_freq, 4)[None, :]
    pos = positions.reshape(t, 1)
    c_pad = jnp.pad(c, ((0, 8 - batch), (0, 0)))

    def row(v):
        return v[None, :]

    h = x.reshape(t, d)
    for l in range(depth):
        mod = _mod_call(c_pad, w_ada[l], row(b_ada[l]))
        mod_rows = mod[:batch].reshape(batch * N_MOD, 1, d)

        w_in_t = w_in[l].T
        ni, nf = t // FFN_TM, D_FF // FFN_TF
        jobs = [
            (ffn2_w_gate[l], (d // ni, FFN_TF), lambda i, j: (i, j), True),
            (ffn2_w_up[l], (d // ni, FFN_TF), lambda i, j: (i, j), True),
            (ffn2_w_down[l], (FFN_TF, d // ni), lambda i, j: (j, i), True),
            (w_in_t, (IN_COLS, d // ni), lambda i, j: (0, i), False),
            (w_out[l], (d // ni, d), lambda i, j: (i, 0), False),
        ]
        h_lead, ffn1_weights = _ffn_call(
            h, mod_rows, 0, row(ffn1_norm[l]), row(final_norm[l]),
            ffn1_w_gate[l], ffn1_w_up[l], ffn1_w_down[l],
            seq=seq, final_norm=False, tm=FFN_TM, tf=FFN_LEAD_TF, lead=True)
        h, (wg2, wu2, wd2, win, wout) = _ffn_call(
            h, mod_rows, 0, row(ffn1_norm[l]), row(final_norm[l]), *ffn1_weights,
            seq=seq, final_norm=False, cast_jobs=jobs, seed=h_lead, tm=FFN_TM, tf=FFN_TF)

        k_rope_w = win[IN_COLS - QK_ROPE:]
        half = QK_ROPE // 2
        wkpe = jnp.concatenate([k_rope_w, k_rope_w[half:], k_rope_w[:half]], axis=0)
        uq = w_uq[l].reshape(Q_RANK, MLA_HEADS, QK_NOPE + QK_ROPE)
        uq_rope = uq[..., QK_NOPE:]
        wuq = jnp.concatenate([uq[..., :QK_NOPE], uq_rope, _half_swap(uq_rope)], axis=-1)
        wuq = wuq.reshape(Q_RANK, MLA_HEADS * QK_PAD).astype(BF16)
        bs_full = jnp.repeat(gmlp_b_s[l].T, GMLP_GROUP, axis=1)
        ukv = w_ukv[l].reshape(KV_RANK, MLA_HEADS, QK_NOPE + V_HEAD)
        wk = ukv[..., :QK_NOPE].reshape(KV_RANK, MLA_HEADS * QK_NOPE).astype(BF16)
        wvt = ukv[..., QK_NOPE:].reshape(KV_RANK, MLA_WIDTH).T.astype(BF16)
        a, q, k, vt = _mix_call(
            h, mod_rows, row(mix_norm[l]), win, wkpe, row(gmlp_v_norm[l]), gmlp_w_s[l].astype(BF16), bs_full,
            row(q_lat_norm[l]), wuq, row(kv_lat_norm[l]), wk, wvt,
            row(q_nope_norm[l]), row(jnp.concatenate([q_rope_norm[l], _half_swap(q_rope_norm[l])])),
            row(k_nope_norm[l]), row(jnp.concatenate([k_rope_norm[l], _half_swap(k_rope_norm[l])])),
            row(out_norm_gmlp[l]), pos, freq, seq=seq)

        m = _attn_call(q, k, vt, batch=batch, seq=seq)
        h = _outproj_call(h, a, m, mod_rows, row(out_norm_mla[l]), wout, seq=seq)

        h, _ = _ffn_call(h, mod_rows, 6, row(ffn2_norm[l]), row(final_norm[l]), wg2, wu2, wd2,
                         seq=seq, final_norm=True, tm=FFN_TM, tf=FFN_TF)
    return h.reshape(batch, seq, d)
```

```python
import functools
import math

import jax
import jax.numpy as jnp
from jax import lax
from jax.experimental import pallas as pl
from jax.experimental.pallas import tpu as pltpu

D_MODEL = 2048
GMLP_WIDTH = 1024
GMLP_GROUP = 128
GMLP_HEADS = GMLP_WIDTH // GMLP_GROUP
CHUNK = 128
V_HEAD = 128
MLA_WIDTH = 1024
MLA_HEADS = MLA_WIDTH // V_HEAD
QK_NOPE = 128
QK_ROPE = 64
Q_RANK = D_MODEL // 4
KV_RANK = D_MODEL // 8
IN_COLS = 2 * GMLP_WIDTH + Q_RANK + KV_RANK + QK_ROPE
D_FF = 5632
ROPE_THETA = 10000.0
EPS = 1e-6
N_MOD = 9
LOG2_E = 1.4426950408889634

LANES = 128
MXU_COLS = 256
QK_PAD = 2 * LANES
IN_COLS_EXT = IN_COLS + QK_ROPE
VMEM_LIMIT = 56 * 1024 * 1024
BF16_SUBLANES = 16
ATTN_SLOTS = 2
ATTN_SUB = 512
FFN_TM = 512
FFN_TF = 512
MIX_SPLIT = 1
FFN_LEAD_TF = 256
FFN_K_SPLIT = 2
FFN_EDGE_SPLIT = 2

F32 = jnp.float32
BF16 = jnp.bfloat16


def _dot(a, b):
    return jnp.dot(a, b, preferred_element_type=F32)


def _dot_nt(a, b):
    return lax.dot_general(a, b, (((1,), (1,)), ((), ())), preferred_element_type=F32)


def _gelu_tanh(x):
    c0 = math.sqrt(2.0 / math.pi)
    hx = 0.5 * x
    return hx + hx * jnp.tanh(x * (c0 + (0.044715 * c0) * (x * x)))


def _rms_rows(x, gain):
    return x * lax.rsqrt(jnp.mean(x * x, axis=-1, keepdims=True) + EPS) * gain


def _const_spec(shape):
    zeros = (0,) * len(shape)
    return pl.BlockSpec(shape, lambda *_: zeros, pipeline_mode=pl.Buffered(1))


def _mod_kernel(c_ref, w_ref, b_ref, o_ref):
    c = c_ref[...]
    c_act = (c * jax.nn.sigmoid(c)).astype(BF16)
    o_ref[...] = _dot(c_act, w_ref[...].astype(BF16)) + b_ref[...]


def _mod_call(c_pad, w_ada, b_ada, *, tn=2048):
    rows, d = c_pad.shape
    n = w_ada.shape[1]
    return pl.pallas_call(
        _mod_kernel,
        grid=(n // tn,),
        in_specs=[
            pl.BlockSpec((rows, d), lambda j: (0, 0)),
            pl.BlockSpec((d, tn), lambda j: (0, j)),
            pl.BlockSpec((1, tn), lambda j: (0, j)),
        ],
        out_specs=pl.BlockSpec((rows, tn), lambda j: (0, j)),
        out_shape=jax.ShapeDtypeStruct((rows, n), F32),
        compiler_params=pltpu.CompilerParams(
            dimension_semantics=("arbitrary",), vmem_limit_bytes=VMEM_LIMIT),
        name="adaln_mod",
    )(c_pad, w_ada, b_ada)


def _ffn_kernel(*refs, final_norm, cast_every_step, lead, seeded):
    n_cast = len(cast_every_step)
    x_ref, sh_ref, sc_ref, g_ref, nw_ref, fin_ref, wg_ref, wu_ref, wd_ref = refs[:9]
    pos = 9
    seed_ref = refs[pos] if seeded else None
    pos += int(seeded)
    cast_in = refs[pos:pos + n_cast]
    pos += n_cast
    o_ref = refs[pos]
    pos += 1
    weight_out = refs[pos:pos + 3] if lead else None
    pos += 3 * int(lead)
    cast_out = refs[pos:pos + n_cast]
    n_scr, acc_scr = refs[pos + n_cast:]
    i = pl.program_id(0)
    j = pl.program_id(1)
    last_j = pl.num_programs(1) - 1

    def cast(every_step):
        for src, dst, flag in zip(cast_in, cast_out, cast_every_step):
            if flag == every_step:
                dst[...] = src[...].astype(BF16)

    cast(True)

    @pl.when(j == 0)
    def _():
        cast(False)

    def step(first, last, n_split):
        if lead:
            wg, wu, wd = (w[...].astype(BF16) for w in (wg_ref, wu_ref, wd_ref))
            for dst, w in zip(weight_out, (wg, wu, wd)):
                dst[...] = w
        else:
            wg, wu, wd = wg_ref[...], wu_ref[...], wd_ref[...]
        sub = x_ref.shape[0] // n_split
        for r in range(n_split):
            rows = slice(r * sub, (r + 1) * sub)
            if first:
                gain = nw_ref[...] * (1.0 + sc_ref[...])
                n = (_rms_rows(x_ref[rows, :], gain) + sh_ref[...]).astype(BF16)
                n_scr[rows, :] = n
            else:
                n = n_scr[rows, :]
            y = None
            wide_enough = wg.shape[1] // FFN_K_SPLIT >= MXU_COLS
            k_split = FFN_K_SPLIT if (n_split == 1 and wide_enough) else 1
            half = wg.shape[1] // k_split
            for c in range(k_split):
                cols = slice(c * half, (c + 1) * half)
                hg = _dot(n, wg[:, cols])
                hu = _dot(n, wu[:, cols])
                hid = (hg * jax.nn.sigmoid(hg) * hu).astype(BF16)
                part = _dot(hid, wd[cols, :])
                y = part if y is None else y + part
            if not first:
                y = acc_scr[rows, :] + y
            if last:
                h = x_ref[rows, :] + (0.5 * g_ref[...]) * y
                if final_norm:
                    h = _rms_rows(h, fin_ref[...])
                o_ref[rows, :] = h
            else:
                acc_scr[rows, :] = y

    def when(cond):
        return pl.when(jnp.logical_and(cond, i > 0) if seeded else cond)

    @when(j == 0)
    def _():
        step(True, False, FFN_EDGE_SPLIT)

    @when(jnp.logical_and(j > 0, j < last_j))
    def _():
        step(False, False, 1)

    @when(j == last_j)
    def _():
        step(False, True, FFN_EDGE_SPLIT)

    if seeded:
        @pl.when(jnp.logical_and(j == last_j, i == 0))
        def _():
            o_ref[...] = seed_ref[...]


def _ffn_call(x, mod_rows, mod_base, norm_w, fin_w, wg, wu, wd, *, seq, final_norm, tm, tf,
              cast_jobs=(), lead=False, seed=None):
    t, d = x.shape
    nf = wg.shape[1] // tf
    blocks_per_batch = seq // tm
    seeded = seed is not None

    def row(k):
        return pl.BlockSpec((None, 1, d), lambda i, j: ((i // blocks_per_batch) * N_MOD + mod_base + k, 0, 0))

    def wj(i, j):
        return jnp.where(i == 0, 0, j) if seeded else j

    cast_arrays = [job[0] for job in cast_jobs]
    cast_specs = [pl.BlockSpec(job[1], job[2]) for job in cast_jobs]
    weight_specs = [
        pl.BlockSpec((d, tf), lambda i, j: (0, wj(i, j))),
        pl.BlockSpec((d, tf), lambda i, j: (0, wj(i, j))),
        pl.BlockSpec((tf, d), lambda i, j: (wj(i, j), 0)),
    ]
    n_rows = tm if lead else t
    outs = pl.pallas_call(
        functools.partial(_ffn_kernel, final_norm=final_norm, lead=lead, seeded=seeded,
                          cast_every_step=tuple(job[3] for job in cast_jobs)),
        grid=(n_rows // tm, nf),
        in_specs=[
            pl.BlockSpec((tm, d), lambda i, j: (i, 0)),
            row(0), row(1), row(2),
            pl.BlockSpec((1, d), lambda i, j: (0, 0)),
            pl.BlockSpec((1, d), lambda i, j: (0, 0)),
            *weight_specs,
            *([pl.BlockSpec((tm, d), lambda i, j: (0, 0))] if seeded else []),
            *cast_specs,
        ],
        out_specs=[pl.BlockSpec((tm, d), lambda i, j: (i, 0)),
                   *(weight_specs if lead else []), *cast_specs],
        out_shape=[jax.ShapeDtypeStruct((n_rows, d), F32),
                   *([jax.ShapeDtypeStruct(w.shape, BF16) for w in (wg, wu, wd)] if lead else []),
                   *(jax.ShapeDtypeStruct(a.shape, BF16) for a in cast_arrays)],
        scratch_shapes=[pltpu.VMEM((tm, d), BF16), pltpu.VMEM((tm, d), F32)],
        compiler_params=pltpu.CompilerParams(
            dimension_semantics=("parallel", "arbitrary"), vmem_limit_bytes=VMEM_LIMIT),
        name="swiglu_ffn_lead" if lead else "swiglu_ffn",
    )(x, mod_rows, mod_rows, mod_rows, norm_w, fin_w, wg, wu, wd,
      *([seed] if seeded else []), *cast_arrays)
    return outs[0], outs[1:]


def _mix_kernel(h_ref, sh_ref, sc_ref, nw_ref, win_ref, wkpe_ref, vn_ref, ws_ref, bs_ref,
                qln_ref, wuq_ref, kvn_ref, wk_ref, wvt_ref, qng_ref, qrg_ref, kng_ref, krg_ref,
                ong_ref, pos_ref, freq_ref,
                a_ref, q_ref, k_ref, vt_ref,
                u_scr, v_scr, a_scr, *, tm):
    o1 = GMLP_WIDTH
    o2 = 2 * GMLP_WIDTH
    o3 = o2 + Q_RANK
    o4 = o3 + KV_RANK
    scale = (QK_NOPE + QK_ROPE) ** -0.5 * LOG2_E
    sub = tm // MIX_SPLIT

    for r in range(MIX_SPLIT):
        tok = slice(r * sub, (r + 1) * sub)
        gain = nw_ref[...] * (1.0 + sc_ref[...])
        n = (_rms_rows(h_ref[tok, :], gain) + sh_ref[...]).astype(BF16)

        u_scr[tok, :] = _gelu_tanh(_dot_nt(n, win_ref[:o1, :]))
        vv = _gelu_tanh(_dot_nt(n, win_ref[o1:o2, :]))
        v_scr[tok, :] = _rms_rows(vv, vn_ref[...]).astype(BF16)
        for c in range(r * sub // CHUNK, (r + 1) * sub // CHUNK):
            rows = slice(c * CHUNK, (c + 1) * CHUNK)
            for g in range(GMLP_HEADS):
                cols = slice(g * GMLP_GROUP, (g + 1) * GMLP_GROUP)
                mixed = _dot(ws_ref[g], v_scr[rows, cols]) + bs_ref[:, cols]
                a_scr[rows, cols] = u_scr[rows, cols] * mixed
        a_ref[tok, :] = _rms_rows(a_scr[tok, :], ong_ref[...]).astype(BF16)

        ang = pos_ref[tok, :].astype(F32) * freq_ref[...]
        lane = lax.broadcasted_iota(jnp.int32, ang.shape, 1)
        sin = jnp.sin(ang)
        rot = jnp.where(lane < QK_ROPE, jnp.cos(ang), jnp.where(lane < QK_ROPE + QK_ROPE // 2, -sin, sin))

        def rope(z, gain):
            w = _rms_rows(z, gain) * rot
            return w + pltpu.roll(w, QK_ROPE, axis=1)

        q_lat = _rms_rows(_dot_nt(n, win_ref[o2:o3, :]), qln_ref[...]).astype(BF16)
        q_nope_gain = qng_ref[...] * scale
        q_rope_gain = qrg_ref[...] * scale
        for hd in range(MLA_HEADS):
            base = hd * QK_PAD
            qq = _dot(q_lat, wuq_ref[:, base:base + QK_PAD])
            q_ref[tok, base:base + LANES] = _rms_rows(qq[:, :LANES], q_nope_gain).astype(BF16)
            q_ref[tok, base + LANES:base + QK_PAD] = rope(qq[:, LANES:], q_rope_gain).astype(BF16)

        k_pe = rope(_dot_nt(n, wkpe_ref[...]), krg_ref[...])
        k_pe = jnp.where(lane < QK_ROPE, k_pe, 0.0).astype(BF16)
        kv_lat = _rms_rows(_dot_nt(n, win_ref[o3:o4, :]), kvn_ref[...]).astype(BF16)
        k_nope = _dot(kv_lat, wk_ref[...])
        for hd in range(MLA_HEADS):
            base = hd * QK_PAD
            cols = slice(hd * QK_NOPE, (hd + 1) * QK_NOPE)
            k_ref[tok, base:base + LANES] = _rms_rows(k_nope[:, cols], kng_ref[...]).astype(BF16)
            k_ref[tok, base + LANES:base + QK_PAD] = k_pe
        vt_ref[:, tok] = _dot_nt(wvt_ref[...], kv_lat).astype(BF16)


def _mix_call(h, mod_rows, mix_norm, win, wkpe, v_norm, ws, bs_full, q_lat_norm, wuq, kv_lat_norm, wk, wvt,
              qn_g, qr_g, kn_g, kr_g, on_g, pos, freq, *, seq, tm=512):
    t, d = h.shape
    blocks_per_batch = seq // tm
    batch = t // seq

    def row(k):
        return pl.BlockSpec((None, 1, d), lambda i: ((i // blocks_per_batch) * N_MOD + 3 + k, 0, 0))

    def tok(width):
        return pl.BlockSpec((tm, width), lambda i: (i, 0))

    return pl.pallas_call(
        functools.partial(_mix_kernel, tm=tm),
        grid=(t // tm,),
        in_specs=[
            tok(d), row(0), row(1),
            _const_spec(mix_norm.shape), _const_spec(win.shape), _const_spec(wkpe.shape),
            _const_spec(v_norm.shape),
            _const_spec(ws.shape), _const_spec(bs_full.shape),
            _const_spec(q_lat_norm.shape), _const_spec(wuq.shape),
            _const_spec(kv_lat_norm.shape), _const_spec(wk.shape), _const_spec(wvt.shape),
            _const_spec(qn_g.shape), _const_spec(qr_g.shape), _const_spec(kn_g.shape),
            _const_spec(kr_g.shape), _const_spec(on_g.shape),
            tok(1), _const_spec(freq.shape),
        ],
        out_specs=[
            tok(GMLP_WIDTH), tok(MLA_HEADS * QK_PAD), tok(MLA_HEADS * QK_PAD),
            pl.BlockSpec((MLA_WIDTH, tm), lambda i: (i // blocks_per_batch, i % blocks_per_batch)),
        ],
        out_shape=[
            jax.ShapeDtypeStruct((t, GMLP_WIDTH), BF16),
            jax.ShapeDtypeStruct((t, MLA_HEADS * QK_PAD), BF16),
            jax.ShapeDtypeStruct((t, MLA_HEADS * QK_PAD), BF16),
            jax.ShapeDtypeStruct((batch * MLA_WIDTH, seq), BF16),
        ],
        scratch_shapes=[
            pltpu.VMEM((tm, GMLP_WIDTH), F32),
            pltpu.VMEM((tm, GMLP_WIDTH), BF16),
            pltpu.VMEM((tm, GMLP_WIDTH), F32),
        ],
        compiler_params=pltpu.CompilerParams(
            dimension_semantics=("parallel",), vmem_limit_bytes=VMEM_LIMIT),
        name="mixer_in",
    )(h, mod_rows, mod_rows, mix_norm, win, wkpe, v_norm, ws, bs_full, q_lat_norm, wuq, kv_lat_norm, wk, wvt,
      qn_g, qr_g, kn_g, kr_g, on_g, pos, freq)


def _attn_kernel(q_ref, k_ref, vt_ref, o_ref, vx_scr, s_scr, p_scr, *, sub, nsub):
    vx_scr[:V_HEAD, :] = vt_ref[...]
    vx_scr[V_HEAD:, :] = jnp.ones((vx_scr.shape[0] - V_HEAD, vx_scr.shape[1]), BF16)

    def rows(i):
        return slice(i * sub, (i + 1) * sub)

    def qk(i):
        s_scr[i % ATTN_SLOTS] = _dot_nt(k_ref[...], q_ref[rows(i), :])

    def softmax(i):
        s = s_scr[i % ATTN_SLOTS]
        p_scr[i % ATTN_SLOTS] = jnp.exp2(s - jnp.max(s, axis=0, keepdims=True)).astype(BF16)

    def pv(i):
        ox = _dot(vx_scr[...], p_scr[i % ATTN_SLOTS])
        o_ref[rows(i), :] = (ox[:V_HEAD, :] / ox[V_HEAD:V_HEAD + 1, :]).T

    qk(0)
    softmax(0)
    qk(1)
    for i in range(nsub - 2):
        qk(i + 2)
        softmax(i + 1)
        pv(i)
    softmax(nsub - 1)
    pv(nsub - 2)
    pv(nsub - 1)


def _attn_call(q, k, vt, *, batch, seq, sub=ATTN_SUB):
    t = q.shape[0]
    nsub = seq // sub
    assert nsub >= 3
    return pl.pallas_call(
        functools.partial(_attn_kernel, sub=sub, nsub=nsub),
        grid=(batch, MLA_HEADS),
        in_specs=[
            pl.BlockSpec((seq, QK_PAD), lambda b, h: (b, h)),
            pl.BlockSpec((seq, QK_PAD), lambda b, h: (b, h)),
            pl.BlockSpec((V_HEAD, seq), lambda b, h: (b * MLA_HEADS + h, 0)),
        ],
        out_specs=pl.BlockSpec((seq, V_HEAD), lambda b, h: (b, h)),
        out_shape=jax.ShapeDtypeStruct((t, MLA_WIDTH), F32),
        scratch_shapes=[
            pltpu.VMEM((V_HEAD + BF16_SUBLANES, seq), BF16),
            pltpu.VMEM((ATTN_SLOTS, seq, sub), F32),
            pltpu.VMEM((ATTN_SLOTS, seq, sub), BF16),
        ],
        compiler_params=pltpu.CompilerParams(
            dimension_semantics=("parallel", "parallel"), vmem_limit_bytes=VMEM_LIMIT),
        name="mla_attention",
    )(q, k, vt)


def _outproj_kernel(h_ref, a_ref, m_ref, g_ref, mg_ref, wo_ref, o_ref):
    m_n = _rms_rows(m_ref[...], mg_ref[...]).astype(BF16)
    y = _dot(a_ref[...], wo_ref[:GMLP_WIDTH, :]) + _dot(m_n, wo_ref[GMLP_WIDTH:, :])
    o_ref[...] = h_ref[...] + g_ref[...] * y


def _outproj_call(h, a, m, mod_rows, mla_gain, wo, *, seq, tm=512):
    t, d = h.shape
    blocks_per_batch = seq // tm
    return pl.pallas_call(
        _outproj_kernel,
        grid=(t // tm,),
        in_specs=[
            pl.BlockSpec((tm, d), lambda i: (i, 0)),
            pl.BlockSpec((tm, GMLP_WIDTH), lambda i: (i, 0)),
            pl.BlockSpec((tm, MLA_WIDTH), lambda i: (i, 0)),
            pl.BlockSpec((None, 1, d), lambda i: ((i // blocks_per_batch) * N_MOD + 5, 0, 0)),
            _const_spec(mla_gain.shape), _const_spec(wo.shape),
        ],
        out_specs=pl.BlockSpec((tm, d), lambda i: (i, 0)),
        out_shape=jax.ShapeDtypeStruct((t, d), F32),
        compiler_params=pltpu.CompilerParams(
            dimension_semantics=("parallel",), vmem_limit_bytes=VMEM_LIMIT),
        name="mixer_out",
    )(h, a, m, mod_rows, mla_gain, wo)


def _half_swap(x):
    half = x.shape[-1] // 2
    return jnp.concatenate([x[..., half:], x[..., :half]], axis=-1)


def kernel(x, c, positions, w_ada, b_ada, ffn1_norm, ffn1_w_gate, ffn1_w_up, ffn1_w_down, mix_norm, w_in, gmlp_v_norm, gmlp_w_s, gmlp_b_s, q_lat_norm, w_uq, kv_lat_norm, w_ukv, q_nope_norm, q_rope_norm, k_nope_norm, k_rope_norm, out_norm_gmlp, out_norm_mla, w_out, ffn2_norm, ffn2_w_gate, ffn2_w_up, ffn2_w_down, final_norm):
    batch, seq, d = x.shape
    depth = w_ada.shape[0]
    t = batch * seq

    inv_freq = ROPE_THETA ** (-jnp.arange(0, QK_ROPE, 2, dtype=F32) / QK_ROPE)
    freq = jnp.tile(inv_freq, 4)[None, :]
    pos = positions.reshape(t, 1)
    c_pad = jnp.pad(c, ((0, 8 - batch), (0, 0)))

    def row(v):
        return v[None, :]

    h = x.reshape(t, d)
    for l in range(depth):
        mod = _mod_call(c_pad, w_ada[l], row(b_ada[l]))
        mod_rows = mod[:batch].reshape(batch * N_MOD, 1, d)

        w_in_t = w_in[l].T
        ni, nf = t // FFN_TM, D_FF // FFN_TF
        jobs = [
            (ffn2_w_gate[l], (d // ni, FFN_TF), lambda i, j: (i, j), True),
            (ffn2_w_up[l], (d // ni, FFN_TF), lambda i, j: (i, j), True),
            (ffn2_w_down[l], (FFN_TF, d // ni), lambda i, j: (j, i), True),
            (w_in_t, (IN_COLS, d // ni), lambda i, j: (0, i), False),
            (w_out[l], (d // ni, d), lambda i, j: (i, 0), False),
        ]
        h_lead, ffn1_weights = _ffn_call(
            h, mod_rows, 0, row(ffn1_norm[l]), row(final_norm[l]),
            ffn1_w_gate[l], ffn1_w_up[l], ffn1_w_down[l],
            seq=seq, final_norm=False, tm=FFN_TM, tf=FFN_LEAD_TF, lead=True)
        h, (wg2, wu2, wd2, win, wout) = _ffn_call(
            h, mod_rows, 0, row(ffn1_norm[l]), row(final_norm[l]), *ffn1_weights,
            seq=seq, final_norm=False, cast_jobs=jobs, seed=h_lead, tm=FFN_TM, tf=FFN_TF)

        k_rope_w = win[IN_COLS - QK_ROPE:]
        half = QK_ROPE // 2
        wkpe = jnp.concatenate([k_rope_w, k_rope_w[half:], k_rope_w[:half]], axis=0)
        uq = w_uq[l].reshape(Q_RANK, MLA_HEADS, QK_NOPE + QK_ROPE)
        uq_rope = uq[..., QK_NOPE:]
        wuq = jnp.concatenate([uq[..., :QK_NOPE], uq_rope, _half_swap(uq_rope)], axis=-1)
        wuq = wuq.reshape(Q_RANK, MLA_HEADS * QK_PAD).astype(BF16)
        bs_full = jnp.repeat(gmlp_b_s[l].T, GMLP_GROUP, axis=1)
        ukv = w_ukv[l].reshape(KV_RANK, MLA_HEADS, QK_NOPE + V_HEAD)
        wk = ukv[..., :QK_NOPE].reshape(KV_RANK, MLA_HEADS * QK_NOPE).astype(BF16)
        wvt = ukv[..., QK_NOPE:].reshape(KV_RANK, MLA_WIDTH).T.astype(BF16)
        a, q, k, vt = _mix_call(
            h, mod_rows, row(mix_norm[l]), win, wkpe, row(gmlp_v_norm[l]), gmlp_w_s[l].astype(BF16), bs_full,
            row(q_lat_norm[l]), wuq, row(kv_lat_norm[l]), wk, wvt,
            row(q_nope_norm[l]), row(jnp.concatenate([q_rope_norm[l], _half_swap(q_rope_norm[l])])),
            row(k_nope_norm[l]), row(jnp.concatenate([k_rope_norm[l], _half_swap(k_rope_norm[l])])),
            row(out_norm_gmlp[l]), pos, freq, seq=seq)

        m = _attn_call(q, k, vt, batch=batch, seq=seq)
        h = _outproj_call(h, a, m, mod_rows, row(out_norm_mla[l]), wout, seq=seq)

        h, _ = _ffn_call(h, mod_rows, 6, row(ffn2_norm[l]), row(final_norm[l]), wg2, wu2, wd2,
                         seq=seq, final_norm=True, tm=FFN_TM, tf=FFN_TF)
    return h.reshape(batch, seq, d)
```

```python
import functools
import math

import jax
import jax.numpy as jnp
from jax import lax
from jax.experimental import pallas as pl
from jax.experimental.pallas import tpu as pltpu

D_MODEL = 2048
GMLP_WIDTH = 1024
GMLP_GROUP = 128
GMLP_HEADS = GMLP_WIDTH // GMLP_GROUP
CHUNK = 128
V_HEAD = 128
MLA_WIDTH = 1024
MLA_HEADS = MLA_WIDTH // V_HEAD
QK_NOPE = 128
QK_ROPE = 64
Q_RANK = D_MODEL // 4
KV_RANK = D_MODEL // 8
IN_COLS = 2 * GMLP_WIDTH + Q_RANK + KV_RANK + QK_ROPE
D_FF = 5632
ROPE_THETA = 10000.0
EPS = 1e-6
N_MOD = 9
LOG2_E = 1.4426950408889634

LANES = 128
MXU_COLS = 256
QK_PAD = 2 * LANES
IN_COLS_EXT = IN_COLS + QK_ROPE
VMEM_LIMIT = 56 * 1024 * 1024
BF16_SUBLANES = 16
ATTN_SLOTS = 2
ATTN_SUB = 512
FFN_TM = 512
FFN_TF = 512
MIX_SPLIT = 1
FFN_LEAD_TF = 256
FFN_LEAD_TM = 1024
FFN_K_SPLIT = 2
FFN_EDGE_SPLIT = 2

F32 = jnp.float32
BF16 = jnp.bfloat16


def _dot(a, b):
    return jnp.dot(a, b, preferred_element_type=F32)


def _dot_nt(a, b):
    return lax.dot_general(a, b, (((1,), (1,)), ((), ())), preferred_element_type=F32)


def _gelu_tanh(x):
    c0 = math.sqrt(2.0 / math.pi)
    hx = 0.5 * x
    return hx + hx * jnp.tanh(x * (c0 + (0.044715 * c0) * (x * x)))


def _rms_rows(x, gain):
    return x * lax.rsqrt(jnp.mean(x * x, axis=-1, keepdims=True) + EPS) * gain


def _const_spec(shape):
    zeros = (0,) * len(shape)
    return pl.BlockSpec(shape, lambda *_: zeros, pipeline_mode=pl.Buffered(1))


def _mod_kernel(c_ref, w_ref, b_ref, o_ref):
    c = c_ref[...]
    c_act = (c * jax.nn.sigmoid(c)).astype(BF16)
    o_ref[...] = _dot(c_act, w_ref[...].astype(BF16)) + b_ref[...]


def _mod_call(c_pad, w_ada, b_ada, *, tn=1024):
    rows, d = c_pad.shape
    n = w_ada.shape[1]
    return pl.pallas_call(
        _mod_kernel,
        grid=(n // tn,),
        in_specs=[
            pl.BlockSpec((rows, d), lambda j: (0, 0)),
            pl.BlockSpec((d, tn), lambda j: (0, j)),
            pl.BlockSpec((1, tn), lambda j: (0, j)),
        ],
        out_specs=pl.BlockSpec((rows, tn), lambda j: (0, j)),
        out_shape=jax.ShapeDtypeStruct((rows, n), F32),
        compiler_params=pltpu.CompilerParams(
            dimension_semantics=("arbitrary",), vmem_limit_bytes=VMEM_LIMIT),
        name="adaln_mod",
    )(c_pad, w_ada, b_ada)


def _ffn_kernel(*refs, final_norm, cast_every_step, lead, n_seed):
    seeded = n_seed > 0
    n_cast = len(cast_every_step)
    x_ref, sh_ref, sc_ref, g_ref, nw_ref, fin_ref, wg_ref, wu_ref, wd_ref = refs[:9]
    pos = 9
    seed_ref = refs[pos] if seeded else None
    pos += int(seeded)
    cast_in = refs[pos:pos + n_cast]
    pos += n_cast
    o_ref = refs[pos]
    pos += 1
    weight_out = refs[pos:pos + 3] if lead else None
    pos += 3 * int(lead)
    cast_out = refs[pos:pos + n_cast]
    n_scr, *acc = refs[pos + n_cast:]
    acc_ref = o_ref if lead else acc[0]
    i = pl.program_id(0)
    j = pl.program_id(1)
    last_j = pl.num_programs(1) - 1

    def cast(every_step):
        for src, dst, flag in zip(cast_in, cast_out, cast_every_step):
            if flag == every_step:
                dst[...] = src[...].astype(BF16)

    cast(True)

    @pl.when(j == 0)
    def _():
        cast(False)

    def step(first, last, n_split):
        if lead:
            wg, wu, wd = (w[...].astype(BF16) for w in (wg_ref, wu_ref, wd_ref))
            for dst, w in zip(weight_out, (wg, wu, wd)):
                dst[...] = w
        else:
            wg, wu, wd = wg_ref[...], wu_ref[...], wd_ref[...]
        sub = x_ref.shape[0] // n_split
        for r in range(n_split):
            rows = slice(r * sub, (r + 1) * sub)
            if first:
                gain = nw_ref[...] * (1.0 + sc_ref[...])
                n = (_rms_rows(x_ref[rows, :], gain) + sh_ref[...]).astype(BF16)
                n_scr[rows, :] = n
            else:
                n = n_scr[rows, :]
            y = None
            wide_enough = wg.shape[1] // FFN_K_SPLIT >= MXU_COLS
            k_split = FFN_K_SPLIT if (n_split == 1 and wide_enough) else 1
            half = wg.shape[1] // k_split
            for c in range(k_split):
                cols = slice(c * half, (c + 1) * half)
                hg = _dot(n, wg[:, cols])
                hu = _dot(n, wu[:, cols])
                hid = (hg * jax.nn.sigmoid(hg) * hu).astype(BF16)
                part = _dot(hid, wd[cols, :])
                y = part if y is None else y + part
            if not first:
                y = acc_ref[rows, :] + y
            if last:
                h = x_ref[rows, :] + (0.5 * g_ref[...]) * y
                if final_norm:
                    h = _rms_rows(h, fin_ref[...])
                o_ref[rows, :] = h
            else:
                acc_ref[rows, :] = y

    def when(cond):
        return pl.when(jnp.logical_and(cond, i >= n_seed) if seeded else cond)

    @when(j == 0)
    def _():
        step(True, False, FFN_EDGE_SPLIT)

    @when(jnp.logical_and(j > 0, j < last_j))
    def _():
        step(False, False, 1)

    @when(j == last_j)
    def _():
        step(False, True, FFN_EDGE_SPLIT)

    if seeded:
        @pl.when(jnp.logical_and(j == last_j, i < n_seed))
        def _():
            o_ref[...] = seed_ref[...]


def _ffn_call(x, mod_rows, mod_base, norm_w, fin_w, wg, wu, wd, *, seq, final_norm, tm, tf,
              cast_jobs=(), lead=False, seed=None):
    t, d = x.shape
    nf = wg.shape[1] // tf
    blocks_per_batch = seq // tm
    n_seed = 0 if seed is None else seed.shape[0] // tm
    seeded = n_seed > 0

    def row(k):
        return pl.BlockSpec((None, 1, d), lambda i, j: ((i // blocks_per_batch) * N_MOD + mod_base + k, 0, 0))

    def wj(i, j):
        return jnp.where(i < n_seed, 0, j) if seeded else j

    cast_arrays = [job[0] for job in cast_jobs]
    cast_specs = [pl.BlockSpec(job[1], job[2]) for job in cast_jobs]
    weight_specs = [
        pl.BlockSpec((d, tf), lambda i, j: (0, wj(i, j))),
        pl.BlockSpec((d, tf), lambda i, j: (0, wj(i, j))),
        pl.BlockSpec((tf, d), lambda i, j: (wj(i, j), 0)),
    ]
    n_rows = tm if lead else t
    outs = pl.pallas_call(
        functools.partial(_ffn_kernel, final_norm=final_norm, lead=lead, n_seed=n_seed,
                          cast_every_step=tuple(job[3] for job in cast_jobs)),
        grid=(n_rows // tm, nf),
        in_specs=[
            pl.BlockSpec((tm, d), lambda i, j: (i, 0), pipeline_mode=pl.Buffered(1 if lead else 2)),
            row(0), row(1), row(2),
            pl.BlockSpec((1, d), lambda i, j: (0, 0)),
            pl.BlockSpec((1, d), lambda i, j: (0, 0)),
            *weight_specs,
            *([pl.BlockSpec((tm, d), lambda i, j: (jnp.minimum(i, n_seed - 1), 0))] if seeded else []),
            *cast_specs,
        ],
        out_specs=[pl.BlockSpec((tm, d), lambda i, j: (i, 0)),
                   *(weight_specs if lead else []), *cast_specs],
        out_shape=[jax.ShapeDtypeStruct((n_rows, d), F32),
                   *([jax.ShapeDtypeStruct(w.shape, BF16) for w in (wg, wu, wd)] if lead else []),
                   *(jax.ShapeDtypeStruct(a.shape, BF16) for a in cast_arrays)],
        scratch_shapes=[pltpu.VMEM((tm, d), BF16), *([] if lead else [pltpu.VMEM((tm, d), F32)])],
        compiler_params=pltpu.CompilerParams(
            dimension_semantics=("parallel", "arbitrary"), vmem_limit_bytes=VMEM_LIMIT),
        name="swiglu_ffn_lead" if lead else "swiglu_ffn",
    )(x, mod_rows, mod_rows, mod_rows, norm_w, fin_w, wg, wu, wd,
      *([seed] if seeded else []), *cast_arrays)
    return outs[0], outs[1:]


def _mix_kernel(h_ref, sh_ref, sc_ref, nw_ref, win_ref, wkpe_ref, vn_ref, ws_ref, bs_ref,
                qln_ref, wuq_ref, kvn_ref, wk_ref, wvt_ref, qng_ref, qrg_ref, kng_ref, krg_ref,
                ong_ref, pos_ref, freq_ref,
                a_ref, q_ref, k_ref, vt_ref,
                u_scr, v_scr, a_scr, *, tm):
    o1 = GMLP_WIDTH
    o2 = 2 * GMLP_WIDTH
    o3 = o2 + Q_RANK
    o4 = o3 + KV_RANK
    scale = (QK_NOPE + QK_ROPE) ** -0.5 * LOG2_E
    sub = tm // MIX_SPLIT

    for r in range(MIX_SPLIT):
        tok = slice(r * sub, (r + 1) * sub)
        gain = nw_ref[...] * (1.0 + sc_ref[...])
        n = (_rms_rows(h_ref[tok, :], gain) + sh_ref[...]).astype(BF16)

        u_scr[tok, :] = _gelu_tanh(_dot_nt(n, win_ref[:o1, :]))
        vv = _gelu_tanh(_dot_nt(n, win_ref[o1:o2, :]))
        v_scr[tok, :] = _rms_rows(vv, vn_ref[...]).astype(BF16)
        for c in range(r * sub // CHUNK, (r + 1) * sub // CHUNK):
            rows = slice(c * CHUNK, (c + 1) * CHUNK)
            for g in range(GMLP_HEADS):
                cols = slice(g * GMLP_GROUP, (g + 1) * GMLP_GROUP)
                mixed = _dot(ws_ref[g], v_scr[rows, cols]) + bs_ref[:, cols]
                a_scr[rows, cols] = u_scr[rows, cols] * mixed
        a_ref[tok, :] = _rms_rows(a_scr[tok, :], ong_ref[...]).astype(BF16)

        ang = pos_ref[tok, :].astype(F32) * freq_ref[...]
        lane = lax.broadcasted_iota(jnp.int32, ang.shape, 1)
        sin = jnp.sin(ang)
        rot = jnp.where(lane < QK_ROPE, jnp.cos(ang), jnp.where(lane < QK_ROPE + QK_ROPE // 2, -sin, sin))

        def rope(z, gain):
            w = _rms_rows(z, gain) * rot
            return w + pltpu.roll(w, QK_ROPE, axis=1)

        q_lat = _rms_rows(_dot_nt(n, win_ref[o2:o3, :]), qln_ref[...]).astype(BF16)
        q_nope_gain = qng_ref[...] * scale
        q_rope_gain = qrg_ref[...] * scale
        for hd in range(MLA_HEADS):
            base = hd * QK_PAD
            qq = _dot(q_lat, wuq_ref[:, base:base + QK_PAD])
            q_ref[tok, base:base + LANES] = _rms_rows(qq[:, :LANES], q_nope_gain).astype(BF16)
            q_ref[tok, base + LANES:base + QK_PAD] = rope(qq[:, LANES:], q_rope_gain).astype(BF16)

        k_pe = rope(_dot_nt(n, wkpe_ref[...]), krg_ref[...])
        k_pe = jnp.where(lane < QK_ROPE, k_pe, 0.0).astype(BF16)
        kv_lat = _rms_rows(_dot_nt(n, win_ref[o3:o4, :]), kvn_ref[...]).astype(BF16)
        k_nope = _dot(kv_lat, wk_ref[...])
        for hd in range(MLA_HEADS):
            base = hd * QK_PAD
            cols = slice(hd * QK_NOPE, (hd + 1) * QK_NOPE)
            k_ref[tok, base:base + LANES] = _rms_rows(k_nope[:, cols], kng_ref[...]).astype(BF16)
            k_ref[tok, base + LANES:base + QK_PAD] = k_pe
        vt_ref[:, tok] = _dot_nt(wvt_ref[...], kv_lat).astype(BF16)


def _mix_call(h, mod_rows, mix_norm, win, wkpe, v_norm, ws, bs_full, q_lat_norm, wuq, kv_lat_norm, wk, wvt,
              qn_g, qr_g, kn_g, kr_g, on_g, pos, freq, *, seq, tm=512):
    t, d = h.shape
    blocks_per_batch = seq // tm
    batch = t // seq

    def row(k):
        return pl.BlockSpec((None, 1, d), lambda i: ((i // blocks_per_batch) * N_MOD + 3 + k, 0, 0))

    def tok(width):
        return pl.BlockSpec((tm, width), lambda i: (i, 0))

    return pl.pallas_call(
        functools.partial(_mix_kernel, tm=tm),
        grid=(t // tm,),
        in_specs=[
            tok(d), row(0), row(1),
            _const_spec(mix_norm.shape), _const_spec(win.shape), _const_spec(wkpe.shape),
            _const_spec(v_norm.shape),
            _const_spec(ws.shape), _const_spec(bs_full.shape),
            _const_spec(q_lat_norm.shape), _const_spec(wuq.shape),
            _const_spec(kv_lat_norm.shape), _const_spec(wk.shape), _const_spec(wvt.shape),
            _const_spec(qn_g.shape), _const_spec(qr_g.shape), _const_spec(kn_g.shape),
            _const_spec(kr_g.shape), _const_spec(on_g.shape),
            tok(1), _const_spec(freq.shape),
        ],
        out_specs=[
            tok(GMLP_WIDTH), tok(MLA_HEADS * QK_PAD), tok(MLA_HEADS * QK_PAD),
            pl.BlockSpec((MLA_WIDTH, tm), lambda i: (i // blocks_per_batch, i % blocks_per_batch)),
        ],
        out_shape=[
            jax.ShapeDtypeStruct((t, GMLP_WIDTH), BF16),
            jax.ShapeDtypeStruct((t, MLA_HEADS * QK_PAD), BF16),
            jax.ShapeDtypeStruct((t, MLA_HEADS * QK_PAD), BF16),
            jax.ShapeDtypeStruct((batch * MLA_WIDTH, seq), BF16),
        ],
        scratch_shapes=[
            pltpu.VMEM((tm, GMLP_WIDTH), F32),
            pltpu.VMEM((tm, GMLP_WIDTH), BF16),
            pltpu.VMEM((tm, GMLP_WIDTH), F32),
        ],
        compiler_params=pltpu.CompilerParams(
            dimension_semantics=("parallel",), vmem_limit_bytes=VMEM_LIMIT),
        name="mixer_in",
    )(h, mod_rows, mod_rows, mix_norm, win, wkpe, v_norm, ws, bs_full, q_lat_norm, wuq, kv_lat_norm, wk, wvt,
      qn_g, qr_g, kn_g, kr_g, on_g, pos, freq)


def _attn_kernel(q_ref, k_ref, vt_ref, o_ref, vx_scr, s_scr, p_scr, *, sub, nsub):
    vx_scr[:V_HEAD, :] = vt_ref[...]
    vx_scr[V_HEAD:, :] = jnp.ones((vx_scr.shape[0] - V_HEAD, vx_scr.shape[1]), BF16)

    def rows(i):
        return slice(i * sub, (i + 1) * sub)

    def qk(i):
        s_scr[i % ATTN_SLOTS] = _dot_nt(k_ref[...], q_ref[rows(i), :])

    def softmax(i):
        s = s_scr[i % ATTN_SLOTS]
        p_scr[i % ATTN_SLOTS] = jnp.exp2(s - jnp.max(s, axis=0, keepdims=True)).astype(BF16)

    def pv(i):
        ox = _dot(vx_scr[...], p_scr[i % ATTN_SLOTS])
        o_ref[rows(i), :] = (ox[:V_HEAD, :] / ox[V_HEAD:V_HEAD + 1, :]).T

    qk(0)
    softmax(0)
    qk(1)
    for i in range(nsub - 2):
        qk(i + 2)
        softmax(i + 1)
        pv(i)
    softmax(nsub - 1)
    pv(nsub - 2)
    pv(nsub - 1)


def _attn_call(q, k, vt, *, batch, seq, sub=ATTN_SUB):
    t = q.shape[0]
    nsub = seq // sub
    assert nsub >= 3
    return pl.pallas_call(
        functools.partial(_attn_kernel, sub=sub, nsub=nsub),
        grid=(batch, MLA_HEADS),
        in_specs=[
            pl.BlockSpec((seq, QK_PAD), lambda b, h: (b, h)),
            pl.BlockSpec((seq, QK_PAD), lambda b, h: (b, h)),
            pl.BlockSpec((V_HEAD, seq), lambda b, h: (b * MLA_HEADS + h, 0)),
        ],
        out_specs=pl.BlockSpec((seq, V_HEAD), lambda b, h: (b, h)),
        out_shape=jax.ShapeDtypeStruct((t, MLA_WIDTH), F32),
        scratch_shapes=[
            pltpu.VMEM((V_HEAD + BF16_SUBLANES, seq), BF16),
            pltpu.VMEM((ATTN_SLOTS, seq, sub), F32),
            pltpu.VMEM((ATTN_SLOTS, seq, sub), BF16),
        ],
        compiler_params=pltpu.CompilerParams(
            dimension_semantics=("parallel", "parallel"), vmem_limit_bytes=VMEM_LIMIT),
        name="mla_attention",
    )(q, k, vt)


def _outproj_kernel(h_ref, a_ref, m_ref, g_ref, mg_ref, wo_ref, o_ref):
    m_n = _rms_rows(m_ref[...], mg_ref[...]).astype(BF16)
    y = _dot(a_ref[...], wo_ref[:GMLP_WIDTH, :]) + _dot(m_n, wo_ref[GMLP_WIDTH:, :])
    o_ref[...] = h_ref[...] + g_ref[...] * y


def _outproj_call(h, a, m, mod_rows, mla_gain, wo, *, seq, tm=512):
    t, d = h.shape
    blocks_per_batch = seq // tm
    return pl.pallas_call(
        _outproj_kernel,
        grid=(t // tm,),
        in_specs=[
            pl.BlockSpec((tm, d), lambda i: (i, 0)),
            pl.BlockSpec((tm, GMLP_WIDTH), lambda i: (i, 0)),
            pl.BlockSpec((tm, MLA_WIDTH), lambda i: (i, 0)),
            pl.BlockSpec((None, 1, d), lambda i: ((i // blocks_per_batch) * N_MOD + 5, 0, 0)),
            _const_spec(mla_gain.shape), _const_spec(wo.shape),
        ],
        out_specs=pl.BlockSpec((tm, d), lambda i: (i, 0)),
        out_shape=jax.ShapeDtypeStruct((t, d), F32),
        compiler_params=pltpu.CompilerParams(
            dimension_semantics=("parallel",), vmem_limit_bytes=VMEM_LIMIT),
        name="mixer_out",
    )(h, a, m, mod_rows, mla_gain, wo)


def _half_swap(x):
    half = x.shape[-1] // 2
    return jnp.concatenate([x[..., half:], x[..., :half]], axis=-1)


def kernel(x, c, positions, w_ada, b_ada, ffn1_norm, ffn1_w_gate, ffn1_w_up, ffn1_w_down, mix_norm, w_in, gmlp_v_norm, gmlp_w_s, gmlp_b_s, q_lat_norm, w_uq, kv_lat_norm, w_ukv, q_nope_norm, q_rope_norm, k_nope_norm, k_rope_norm, out_norm_gmlp, out_norm_mla, w_out, ffn2_norm, ffn2_w_gate, ffn2_w_up, ffn2_w_down, final_norm):
    batch, seq, d = x.shape
    depth = w_ada.shape[0]
    t = batch * seq

    inv_freq = ROPE_THETA ** (-jnp.arange(0, QK_ROPE, 2, dtype=F32) / QK_ROPE)
    freq = jnp.tile(inv_freq, 4)[None, :]
    pos = positions.reshape(t, 1)
    c_pad = jnp.pad(c, ((0, 8 - batch), (0, 0)))

    def row(v):
        return v[None, :]

    h = x.reshape(t, d)
    for l in range(depth):
        mod = _mod_call(c_pad, w_ada[l], row(b_ada[l]))
        mod_rows = mod[:batch].reshape(batch * N_MOD, 1, d)

        w_in_t = w_in[l].T
        ni, nf = t // FFN_TM, D_FF // FFN_TF
        jobs = [
            (ffn2_w_gate[l], (d // ni, FFN_TF), lambda i, j: (i, j), True),
            (ffn2_w_up[l], (d // ni, FFN_TF), lambda i, j: (i, j), True),
            (ffn2_w_down[l], (FFN_TF, d // ni), lambda i, j: (j, i), True),
            (w_in_t, (IN_COLS, d // ni), lambda i, j: (0, i), False),
            (w_out[l], (d // ni, d), lambda i, j: (i, 0), False),
        ]
        h_lead, ffn1_weights = _ffn_call(
            h, mod_rows, 0, row(ffn1_norm[l]), row(final_norm[l]),
            ffn1_w_gate[l], ffn1_w_up[l], ffn1_w_down[l],
            seq=seq, final_norm=False, tm=FFN_LEAD_TM, tf=FFN_LEAD_TF, lead=True)
        h, (wg2, wu2, wd2, win, wout) = _ffn_call(
            h, mod_rows, 0, row(ffn1_norm[l]), row(final_norm[l]), *ffn1_weights,
            seq=seq, final_norm=False, cast_jobs=jobs, seed=h_lead, tm=FFN_TM, tf=FFN_TF)

        k_rope_w = win[IN_COLS - QK_ROPE:]
        half = QK_ROPE // 2
        wkpe = jnp.concatenate([k_rope_w, k_rope_w[half:], k_rope_w[:half]], axis=0)
        uq = w_uq[l].reshape(Q_RANK, MLA_HEADS, QK_NOPE + QK_ROPE)
        uq_rope = uq[..., QK_NOPE:]
        wuq = jnp.concatenate([uq[..., :QK_NOPE], uq_rope, _half_swap(uq_rope)], axis=-1)
        wuq = wuq.reshape(Q_RANK, MLA_HEADS * QK_PAD).astype(BF16)
        bs_full = jnp.repeat(gmlp_b_s[l].T, GMLP_GROUP, axis=1)
        ukv = w_ukv[l].reshape(KV_RANK, MLA_HEADS, QK_NOPE + V_HEAD)
        wk = ukv[..., :QK_NOPE].reshape(KV_RANK, MLA_HEADS * QK_NOPE).astype(BF16)
        wvt = ukv[..., QK_NOPE:].reshape(KV_RANK, MLA_WIDTH).T.astype(BF16)
        a, q, k, vt = _mix_call(
            h, mod_rows, row(mix_norm[l]), win, wkpe, row(gmlp_v_norm[l]), gmlp_w_s[l].astype(BF16), bs_full,
            row(q_lat_norm[l]), wuq, row(kv_lat_norm[l]), wk, wvt,
            row(q_nope_norm[l]), row(jnp.concatenate([q_rope_norm[l], _half_swap(q_rope_norm[l])])),
            row(k_nope_norm[l]), row(jnp.concatenate([k_rope_norm[l], _half_swap(k_rope_norm[l])])),
            row(out_norm_gmlp[l]), pos, freq, seq=seq)

        m = _attn_call(q, k, vt, batch=batch, seq=seq)
        h = _outproj_call(h, a, m, mod_rows, row(out_norm_mla[l]), wout, seq=seq)

        h, _ = _ffn_call(h, mod_rows, 6, row(ffn2_norm[l]), row(final_norm[l]), wg2, wu2, wd2,
                         seq=seq, final_norm=True, tm=FFN_TM, tf=FFN_TF)
    return h.reshape(batch, seq, d)
```

```python
import functools
import math

import jax
import jax.numpy as jnp
from jax import lax
from jax.experimental import pallas as pl
from jax.experimental.pallas import tpu as pltpu

D_MODEL = 2048
GMLP_WIDTH = 1024
GMLP_GROUP = 128
GMLP_HEADS = GMLP_WIDTH // GMLP_GROUP
CHUNK = 128
V_HEAD = 128
MLA_WIDTH = 1024
MLA_HEADS = MLA_WIDTH // V_HEAD
QK_NOPE = 128
QK_ROPE = 64
Q_RANK = D_MODEL // 4
KV_RANK = D_MODEL // 8
IN_COLS = 2 * GMLP_WIDTH + Q_RANK + KV_RANK + QK_ROPE
D_FF = 5632
ROPE_THETA = 10000.0
EPS = 1e-6
N_MOD = 9
LOG2_E = 1.4426950408889634

LANES = 128
MXU_COLS = 256
QK_PAD = 2 * LANES
IN_COLS_EXT = IN_COLS + QK_ROPE
VMEM_LIMIT = 56 * 1024 * 1024
BF16_SUBLANES = 16
ATTN_SLOTS = 2
ATTN_SUB = 512
FFN_TM = 512
FFN_TF = 512
FFN2_TF = 1024
MIX_SPLIT = 1
FFN_LEAD_TF = 256
FFN_LEAD_TM = 1024
FFN_K_SPLIT = 2
FFN_EDGE_SPLIT = 2

F32 = jnp.float32
BF16 = jnp.bfloat16


def _dot(a, b):
    return jnp.dot(a, b, preferred_element_type=F32)


def _dot_nt(a, b):
    return lax.dot_general(a, b, (((1,), (1,)), ((), ())), preferred_element_type=F32)


def _gelu_tanh(x):
    c0 = math.sqrt(2.0 / math.pi)
    hx = 0.5 * x
    return hx + hx * jnp.tanh(x * (c0 + (0.044715 * c0) * (x * x)))


def _rms_rows(x, gain):
    return x * lax.rsqrt(jnp.mean(x * x, axis=-1, keepdims=True) + EPS) * gain


def _const_spec(shape):
    zeros = (0,) * len(shape)
    return pl.BlockSpec(shape, lambda *_: zeros, pipeline_mode=pl.Buffered(1))


def _mod_kernel(c_ref, w_ref, b_ref, o_ref):
    c = c_ref[...]
    c_act = (c * jax.nn.sigmoid(c)).astype(BF16)
    o_ref[...] = _dot(c_act, w_ref[...].astype(BF16)) + b_ref[...]


def _mod_call(c_pad, w_ada, b_ada, *, tn=1024):
    rows, d = c_pad.shape
    n = w_ada.shape[1]
    return pl.pallas_call(
        _mod_kernel,
        grid=(n // tn,),
        in_specs=[
            pl.BlockSpec((rows, d), lambda j: (0, 0)),
            pl.BlockSpec((d, tn), lambda j: (0, j)),
            pl.BlockSpec((1, tn), lambda j: (0, j)),
        ],
        out_specs=pl.BlockSpec((rows, tn), lambda j: (0, j)),
        out_shape=jax.ShapeDtypeStruct((rows, n), F32),
        compiler_params=pltpu.CompilerParams(
            dimension_semantics=("arbitrary",), vmem_limit_bytes=VMEM_LIMIT),
        name="adaln_mod",
    )(c_pad, w_ada, b_ada)


def _ffn_kernel(*refs, final_norm, cast_every_step, lead, n_seed, last_width):
    seeded = n_seed > 0
    n_cast = len(cast_every_step)
    x_ref, sh_ref, sc_ref, g_ref, nw_ref, fin_ref, wg_ref, wu_ref, wd_ref = refs[:9]
    pos = 9
    seed_ref = refs[pos] if seeded else None
    pos += int(seeded)
    cast_in = refs[pos:pos + n_cast]
    pos += n_cast
    o_ref = refs[pos]
    pos += 1
    weight_out = refs[pos:pos + 3] if lead else None
    pos += 3 * int(lead)
    cast_out = refs[pos:pos + n_cast]
    n_scr = refs[pos + n_cast]
    acc_ref = o_ref
    i = pl.program_id(0)
    j = pl.program_id(1)
    last_j = pl.num_programs(1) - 1

    def cast(every_step):
        for src, dst, flag in zip(cast_in, cast_out, cast_every_step):
            if flag == every_step:
                dst[...] = src[...].astype(BF16)

    cast(True)

    @pl.when(j == 0)
    def _():
        cast(False)

    def step(first, last, n_split):
        if lead:
            wg, wu, wd = (w[...].astype(BF16) for w in (wg_ref, wu_ref, wd_ref))
            for dst, w in zip(weight_out, (wg, wu, wd)):
                dst[...] = w
        else:
            width = last_width if last else wg_ref.shape[1]
            wg, wu, wd = wg_ref[:, :width], wu_ref[:, :width], wd_ref[:width, :]
        sub = x_ref.shape[0] // n_split
        for r in range(n_split):
            rows = slice(r * sub, (r + 1) * sub)
            if first:
                gain = nw_ref[...] * (1.0 + sc_ref[...])
                n = (_rms_rows(x_ref[rows, :], gain) + sh_ref[...]).astype(BF16)
                n_scr[rows, :] = n
            else:
                n = n_scr[rows, :]
            y = None
            wide_enough = wg.shape[1] // FFN_K_SPLIT >= MXU_COLS
            k_split = FFN_K_SPLIT if (n_split == 1 and wide_enough) else 1
            half = wg.shape[1] // k_split
            for c in range(k_split):
                cols = slice(c * half, (c + 1) * half)
                hg = _dot(n, wg[:, cols])
                hu = _dot(n, wu[:, cols])
                hid = (hg * jax.nn.sigmoid(hg) * hu).astype(BF16)
                part = _dot(hid, wd[cols, :])
                y = part if y is None else y + part
            if not first:
                y = acc_ref[rows, :] + y
            if last:
                h = x_ref[rows, :] + (0.5 * g_ref[...]) * y
                if final_norm:
                    h = _rms_rows(h, fin_ref[...])
                o_ref[rows, :] = h
            else:
                acc_ref[rows, :] = y

    def when(cond):
        return pl.when(jnp.logical_and(cond, i >= n_seed) if seeded else cond)

    @when(j == 0)
    def _():
        step(True, False, FFN_EDGE_SPLIT)

    @when(jnp.logical_and(j > 0, j < last_j))
    def _():
        step(False, False, 1)

    @when(j == last_j)
    def _():
        step(False, True, FFN_EDGE_SPLIT)

    if seeded:
        @pl.when(jnp.logical_and(j == last_j, i < n_seed))
        def _():
            o_ref[...] = seed_ref[...]


def _ffn_call(x, mod_rows, mod_base, norm_w, fin_w, wg, wu, wd, *, seq, final_norm, tm, tf,
              cast_jobs=(), lead=False, seed=None):
    t, d = x.shape
    d_ff = wg.shape[1]
    nf = pl.cdiv(d_ff, tf)
    last_width = d_ff - (nf - 1) * tf
    assert not lead or last_width == tf
    blocks_per_batch = seq // tm
    n_seed = 0 if seed is None else seed.shape[0] // tm
    seeded = n_seed > 0

    def row(k):
        return pl.BlockSpec((None, 1, d), lambda i, j: ((i // blocks_per_batch) * N_MOD + mod_base + k, 0, 0))

    def wj(i, j):
        return jnp.where(i < n_seed, 0, j) if seeded else j

    cast_arrays = [job[0] for job in cast_jobs]
    cast_specs = [pl.BlockSpec(job[1], job[2]) for job in cast_jobs]
    weight_specs = [
        pl.BlockSpec((d, tf), lambda i, j: (0, wj(i, j))),
        pl.BlockSpec((d, tf), lambda i, j: (0, wj(i, j))),
        pl.BlockSpec((tf, d), lambda i, j: (wj(i, j), 0)),
    ]
    n_rows = tm if lead else t
    outs = pl.pallas_call(
        functools.partial(_ffn_kernel, final_norm=final_norm, lead=lead, n_seed=n_seed,
                          last_width=last_width,
                          cast_every_step=tuple(job[3] for job in cast_jobs)),
        grid=(n_rows // tm, nf),
        in_specs=[
            pl.BlockSpec((tm, d), lambda i, j: (i, 0), pipeline_mode=pl.Buffered(1 if lead else 2)),
            row(0), row(1), row(2),
            pl.BlockSpec((1, d), lambda i, j: (0, 0)),
            pl.BlockSpec((1, d), lambda i, j: (0, 0)),
            *weight_specs,
            *([pl.BlockSpec((tm, d), lambda i, j: (jnp.minimum(i, n_seed - 1), 0))] if seeded else []),
            *cast_specs,
        ],
        out_specs=[pl.BlockSpec((tm, d), lambda i, j: (i, 0)),
                   *(weight_specs if lead else []), *cast_specs],
        out_shape=[jax.ShapeDtypeStruct((n_rows, d), F32),
                   *([jax.ShapeDtypeStruct(w.shape, BF16) for w in (wg, wu, wd)] if lead else []),
                   *(jax.ShapeDtypeStruct(a.shape, BF16) for a in cast_arrays)],
        scratch_shapes=[pltpu.VMEM((tm, d), BF16)],
        compiler_params=pltpu.CompilerParams(
            dimension_semantics=("parallel", "arbitrary"), vmem_limit_bytes=VMEM_LIMIT),
        name="swiglu_ffn_lead" if lead else "swiglu_ffn",
    )(x, mod_rows, mod_rows, mod_rows, norm_w, fin_w, wg, wu, wd,
      *([seed] if seeded else []), *cast_arrays)
    return outs[0], outs[1:]


def _mix_kernel(h_ref, sh_ref, sc_ref, nw_ref, win_ref, wkpe_ref, vn_ref, ws_ref, bs_ref,
                qln_ref, wuq_ref, kvn_ref, wk_ref, wvt_ref, qng_ref, qrg_ref, kng_ref, krg_ref,
                ong_ref, pos_ref, freq_ref,
                a_ref, q_ref, k_ref, vt_ref,
                u_scr, v_scr, a_scr, *, tm):
    o1 = GMLP_WIDTH
    o2 = 2 * GMLP_WIDTH
    o3 = o2 + Q_RANK
    o4 = o3 + KV_RANK
    scale = (QK_NOPE + QK_ROPE) ** -0.5 * LOG2_E
    sub = tm // MIX_SPLIT

    for r in range(MIX_SPLIT):
        tok = slice(r * sub, (r + 1) * sub)
        gain = nw_ref[...] * (1.0 + sc_ref[...])
        n = (_rms_rows(h_ref[tok, :], gain) + sh_ref[...]).astype(BF16)

        u_scr[tok, :] = _gelu_tanh(_dot_nt(n, win_ref[:o1, :]))
        vv = _gelu_tanh(_dot_nt(n, win_ref[o1:o2, :]))
        v_scr[tok, :] = _rms_rows(vv, vn_ref[...]).astype(BF16)
        for c in range(r * sub // CHUNK, (r + 1) * sub // CHUNK):
            rows = slice(c * CHUNK, (c + 1) * CHUNK)
            for g in range(GMLP_HEADS):
                cols = slice(g * GMLP_GROUP, (g + 1) * GMLP_GROUP)
                mixed = _dot(ws_ref[g], v_scr[rows, cols]) + bs_ref[:, cols]
                a_scr[rows, cols] = u_scr[rows, cols] * mixed
        a_ref[tok, :] = _rms_rows(a_scr[tok, :], ong_ref[...]).astype(BF16)

        ang = pos_ref[tok, :].astype(F32) * freq_ref[...]
        lane = lax.broadcasted_iota(jnp.int32, ang.shape, 1)
        sin = jnp.sin(ang)
        rot = jnp.where(lane < QK_ROPE, jnp.cos(ang), jnp.where(lane < QK_ROPE + QK_ROPE // 2, -sin, sin))

        def rope(z, gain):
            w = _rms_rows(z, gain) * rot
            return w + pltpu.roll(w, QK_ROPE, axis=1)

        q_lat = _rms_rows(_dot_nt(n, win_ref[o2:o3, :]), qln_ref[...]).astype(BF16)
        q_nope_gain = qng_ref[...] * scale
        q_rope_gain = qrg_ref[...] * scale
        for hd in range(MLA_HEADS):
            base = hd * QK_PAD
            qq = _dot(q_lat, wuq_ref[:, base:base + QK_PAD])
            q_ref[tok, base:base + LANES] = _rms_rows(qq[:, :LANES], q_nope_gain).astype(BF16)
            q_ref[tok, base + LANES:base + QK_PAD] = rope(qq[:, LANES:], q_rope_gain).astype(BF16)

        k_pe = rope(_dot_nt(n, wkpe_ref[...]), krg_ref[...])
        k_pe = jnp.where(lane < QK_ROPE, k_pe, 0.0).astype(BF16)
        kv_lat = _rms_rows(_dot_nt(n, win_ref[o3:o4, :]), kvn_ref[...]).astype(BF16)
        k_nope = _dot(kv_lat, wk_ref[...])
        for hd in range(MLA_HEADS):
            base = hd * QK_PAD
            cols = slice(hd * QK_NOPE, (hd + 1) * QK_NOPE)
            k_ref[tok, base:base + LANES] = _rms_rows(k_nope[:, cols], kng_ref[...]).astype(BF16)
            k_ref[tok, base + LANES:base + QK_PAD] = k_pe
        vt_ref[:, tok] = _dot_nt(wvt_ref[...], kv_lat).astype(BF16)


def _mix_call(h, mod_rows, mix_norm, win, wkpe, v_norm, ws, bs_full, q_lat_norm, wuq, kv_lat_norm, wk, wvt,
              qn_g, qr_g, kn_g, kr_g, on_g, pos, freq, *, seq, tm=512):
    t, d = h.shape
    blocks_per_batch = seq // tm
    batch = t // seq

    def row(k):
        return pl.BlockSpec((None, 1, d), lambda i: ((i // blocks_per_batch) * N_MOD + 3 + k, 0, 0))

    def tok(width):
        return pl.BlockSpec((tm, width), lambda i: (i, 0))

    return pl.pallas_call(
        functools.partial(_mix_kernel, tm=tm),
        grid=(t // tm,),
        in_specs=[
            tok(d), row(0), row(1),
            _const_spec(mix_norm.shape), _const_spec(win.shape), _const_spec(wkpe.shape),
            _const_spec(v_norm.shape),
            _const_spec(ws.shape), _const_spec(bs_full.shape),
            _const_spec(q_lat_norm.shape), _const_spec(wuq.shape),
            _const_spec(kv_lat_norm.shape), _const_spec(wk.shape), _const_spec(wvt.shape),
            _const_spec(qn_g.shape), _const_spec(qr_g.shape), _const_spec(kn_g.shape),
            _const_spec(kr_g.shape), _const_spec(on_g.shape),
            tok(1), _const_spec(freq.shape),
        ],
        out_specs=[
            tok(GMLP_WIDTH), tok(MLA_HEADS * QK_PAD), tok(MLA_HEADS * QK_PAD),
            pl.BlockSpec((MLA_WIDTH, tm), lambda i: (i // blocks_per_batch, i % blocks_per_batch)),
        ],
        out_shape=[
            jax.ShapeDtypeStruct((t, GMLP_WIDTH), BF16),
            jax.ShapeDtypeStruct((t, MLA_HEADS * QK_PAD), BF16),
            jax.ShapeDtypeStruct((t, MLA_HEADS * QK_PAD), BF16),
            jax.ShapeDtypeStruct((batch * MLA_WIDTH, seq), BF16),
        ],
        scratch_shapes=[
            pltpu.VMEM((tm, GMLP_WIDTH), F32),
            pltpu.VMEM((tm, GMLP_WIDTH), BF16),
            pltpu.VMEM((tm, GMLP_WIDTH), F32),
        ],
        compiler_params=pltpu.CompilerParams(
            dimension_semantics=("parallel",), vmem_limit_bytes=VMEM_LIMIT),
        name="mixer_in",
    )(h, mod_rows, mod_rows, mix_norm, win, wkpe, v_norm, ws, bs_full, q_lat_norm, wuq, kv_lat_norm, wk, wvt,
      qn_g, qr_g, kn_g, kr_g, on_g, pos, freq)


def _attn_kernel(q_ref, k_ref, vt_ref, o_ref, vx_scr, s_scr, p_scr, *, sub, nsub):
    vx_scr[:V_HEAD, :] = vt_ref[...]
    vx_scr[V_HEAD:, :] = jnp.ones((vx_scr.shape[0] - V_HEAD, vx_scr.shape[1]), BF16)

    def rows(i):
        return slice(i * sub, (i + 1) * sub)

    def qk(i):
        s_scr[i % ATTN_SLOTS] = _dot_nt(k_ref[...], q_ref[rows(i), :])

    def softmax(i):
        s = s_scr[i % ATTN_SLOTS]
        p_scr[i % ATTN_SLOTS] = jnp.exp2(s - jnp.max(s, axis=0, keepdims=True)).astype(BF16)

    def pv(i):
        ox = _dot(vx_scr[...], p_scr[i % ATTN_SLOTS])
        o_ref[rows(i), :] = (ox[:V_HEAD, :] / ox[V_HEAD:V_HEAD + 1, :]).T

    qk(0)
    softmax(0)
    qk(1)
    for i in range(nsub - 2):
        qk(i + 2)
        softmax(i + 1)
        pv(i)
    softmax(nsub - 1)
    pv(nsub - 2)
    pv(nsub - 1)


def _attn_call(q, k, vt, *, batch, seq, sub=ATTN_SUB):
    t = q.shape[0]
    nsub = seq // sub
    assert nsub >= 3
    return pl.pallas_call(
        functools.partial(_attn_kernel, sub=sub, nsub=nsub),
        grid=(batch, MLA_HEADS),
        in_specs=[
            pl.BlockSpec((seq, QK_PAD), lambda b, h: (b, h)),
            pl.BlockSpec((seq, QK_PAD), lambda b, h: (b, h)),
            pl.BlockSpec((V_HEAD, seq), lambda b, h: (b * MLA_HEADS + h, 0)),
        ],
        out_specs=pl.BlockSpec((seq, V_HEAD), lambda b, h: (b, h)),
        out_shape=jax.ShapeDtypeStruct((t, MLA_WIDTH), F32),
        scratch_shapes=[
            pltpu.VMEM((V_HEAD + BF16_SUBLANES, seq), BF16),
            pltpu.VMEM((ATTN_SLOTS, seq, sub), F32),
            pltpu.VMEM((ATTN_SLOTS, seq, sub), BF16),
        ],
        compiler_params=pltpu.CompilerParams(
            dimension_semantics=("parallel", "parallel"), vmem_limit_bytes=VMEM_LIMIT),
        name="mla_attention",
    )(q, k, vt)


def _outproj_kernel(h_ref, a_ref, m_ref, g_ref, mg_ref, wo_ref, o_ref):
    m_n = _rms_rows(m_ref[...], mg_ref[...]).astype(BF16)
    y = _dot(a_ref[...], wo_ref[:GMLP_WIDTH, :]) + _dot(m_n, wo_ref[GMLP_WIDTH:, :])
    o_ref[...] = h_ref[...] + g_ref[...] * y


def _outproj_call(h, a, m, mod_rows, mla_gain, wo, *, seq, tm=512):
    t, d = h.shape
    blocks_per_batch = seq // tm
    return pl.pallas_call(
        _outproj_kernel,
        grid=(t // tm,),
        in_specs=[
            pl.BlockSpec((tm, d), lambda i: (i, 0)),
            pl.BlockSpec((tm, GMLP_WIDTH), lambda i: (i, 0)),
            pl.BlockSpec((tm, MLA_WIDTH), lambda i: (i, 0)),
            pl.BlockSpec((None, 1, d), lambda i: ((i // blocks_per_batch) * N_MOD + 5, 0, 0)),
            _const_spec(mla_gain.shape), _const_spec(wo.shape),
        ],
        out_specs=pl.BlockSpec((tm, d), lambda i: (i, 0)),
        out_shape=jax.ShapeDtypeStruct((t, d), F32),
        compiler_params=pltpu.CompilerParams(
            dimension_semantics=("parallel",), vmem_limit_bytes=VMEM_LIMIT),
        name="mixer_out",
    )(h, a, m, mod_rows, mla_gain, wo)


def _half_swap(x):
    half = x.shape[-1] // 2
    return jnp.concatenate([x[..., half:], x[..., :half]], axis=-1)


def kernel(x, c, positions, w_ada, b_ada, ffn1_norm, ffn1_w_gate, ffn1_w_up, ffn1_w_down, mix_norm, w_in, gmlp_v_norm, gmlp_w_s, gmlp_b_s, q_lat_norm, w_uq, kv_lat_norm, w_ukv, q_nope_norm, q_rope_norm, k_nope_norm, k_rope_norm, out_norm_gmlp, out_norm_mla, w_out, ffn2_norm, ffn2_w_gate, ffn2_w_up, ffn2_w_down, final_norm):
    batch, seq, d = x.shape
    depth = w_ada.shape[0]
    t = batch * seq

    inv_freq = ROPE_THETA ** (-jnp.arange(0, QK_ROPE, 2, dtype=F32) / QK_ROPE)
    freq = jnp.tile(inv_freq, 4)[None, :]
    pos = positions.reshape(t, 1)
    c_pad = jnp.pad(c, ((0, 8 - batch), (0, 0)))

    def row(v):
        return v[None, :]

    h = x.reshape(t, d)
    for l in range(depth):
        mod = _mod_call(c_pad, w_ada[l], row(b_ada[l]))
        mod_rows = mod[:batch].reshape(batch * N_MOD, 1, d)

        w_in_t = w_in[l].T
        ni, nf = t // FFN_TM, D_FF // FFN_TF
        jobs = [
            (ffn2_w_gate[l], (d // ni, FFN_TF), lambda i, j: (i, j), True),
            (ffn2_w_up[l], (d // ni, FFN_TF), lambda i, j: (i, j), True),
            (ffn2_w_down[l], (FFN_TF, d // ni), lambda i, j: (j, i), True),
            (w_in_t, (IN_COLS, d // ni), lambda i, j: (0, i), False),
            (w_out[l], (d // ni, d), lambda i, j: (i, 0), False),
        ]
        h_lead, ffn1_weights = _ffn_call(
            h, mod_rows, 0, row(ffn1_norm[l]), row(final_norm[l]),
            ffn1_w_gate[l], ffn1_w_up[l], ffn1_w_down[l],
            seq=seq, final_norm=False, tm=FFN_LEAD_TM, tf=FFN_LEAD_TF, lead=True)
        h, (wg2, wu2, wd2, win, wout) = _ffn_call(
            h, mod_rows, 0, row(ffn1_norm[l]), row(final_norm[l]), *ffn1_weights,
            seq=seq, final_norm=False, cast_jobs=jobs, seed=h_lead, tm=FFN_TM, tf=FFN_TF)

        k_rope_w = win[IN_COLS - QK_ROPE:]
        half = QK_ROPE // 2
        wkpe = jnp.concatenate([k_rope_w, k_rope_w[half:], k_rope_w[:half]], axis=0)
        uq = w_uq[l].reshape(Q_RANK, MLA_HEADS, QK_NOPE + QK_ROPE)
        uq_rope = uq[..., QK_NOPE:]
        wuq = jnp.concatenate([uq[..., :QK_NOPE], uq_rope, _half_swap(uq_rope)], axis=-1)
        wuq = wuq.reshape(Q_RANK, MLA_HEADS * QK_PAD).astype(BF16)
        bs_full = jnp.repeat(gmlp_b_s[l].T, GMLP_GROUP, axis=1)
        ukv = w_ukv[l].reshape(KV_RANK, MLA_HEADS, QK_NOPE + V_HEAD)
        wk = ukv[..., :QK_NOPE].reshape(KV_RANK, MLA_HEADS * QK_NOPE).astype(BF16)
        wvt = ukv[..., QK_NOPE:].reshape(KV_RANK, MLA_WIDTH).T.astype(BF16)
        a, q, k, vt = _mix_call(
            h, mod_rows, row(mix_norm[l]), win, wkpe, row(gmlp_v_norm[l]), gmlp_w_s[l].astype(BF16), bs_full,
            row(q_lat_norm[l]), wuq, row(kv_lat_norm[l]), wk, wvt,
            row(q_nope_norm[l]), row(jnp.concatenate([q_rope_norm[l], _half_swap(q_rope_norm[l])])),
            row(k_nope_norm[l]), row(jnp.concatenate([k_rope_norm[l], _half_swap(k_rope_norm[l])])),
            row(out_norm_gmlp[l]), pos, freq, seq=seq)

        m = _attn_call(q, k, vt, batch=batch, seq=seq)
        h = _outproj_call(h, a, m, mod_rows, row(out_norm_mla[l]), wout, seq=seq)

        h, _ = _ffn_call(h, mod_rows, 6, row(ffn2_norm[l]), row(final_norm[l]), wg2, wu2, wd2,
                         seq=seq, final_norm=True, tm=FFN_TM, tf=FFN2_TF)
    return h.reshape(batch, seq, d)
```

```python
import functools
import math

import jax
import jax.numpy as jnp
from jax import lax
from jax.experimental import pallas as pl
from jax.experimental.pallas import tpu as pltpu

D_MODEL = 2048
GMLP_WIDTH = 1024
GMLP_GROUP = 128
GMLP_HEADS = GMLP_WIDTH // GMLP_GROUP
CHUNK = 128
V_HEAD = 128
MLA_WIDTH = 1024
MLA_HEADS = MLA_WIDTH // V_HEAD
QK_NOPE = 128
QK_ROPE = 64
Q_RANK = D_MODEL // 4
KV_RANK = D_MODEL // 8
IN_COLS = 2 * GMLP_WIDTH + Q_RANK + KV_RANK + QK_ROPE
D_FF = 5632
ROPE_THETA = 10000.0
EPS = 1e-6
N_MOD = 9
LOG2_E = 1.4426950408889634

LANES = 128
MXU_COLS = 256
QK_PAD = 2 * LANES
IN_COLS_EXT = IN_COLS + QK_ROPE
VMEM_LIMIT = 56 * 1024 * 1024
BF16_SUBLANES = 16
ATTN_SLOTS = 2
ATTN_SUB = 512
FFN_TM = 512
FFN_TF = 512
MIX_SPLIT = 1
FFN_LEAD_TF = 256
FFN_LEAD_TM = 1024
FFN_K_SPLIT = 2
FFN_EDGE_SPLIT = 2

F32 = jnp.float32
BF16 = jnp.bfloat16


def _dot(a, b):
    return jnp.dot(a, b, preferred_element_type=F32)


def _dot_nt(a, b):
    return lax.dot_general(a, b, (((1,), (1,)), ((), ())), preferred_element_type=F32)


def _gelu_tanh(x):
    c0 = math.sqrt(2.0 / math.pi)
    hx = 0.5 * x
    return hx + hx * jnp.tanh(x * (c0 + (0.044715 * c0) * (x * x)))


def _rms_rows(x, gain):
    return x * lax.rsqrt(jnp.mean(x * x, axis=-1, keepdims=True) + EPS) * gain


def _const_spec(shape):
    zeros = (0,) * len(shape)
    return pl.BlockSpec(shape, lambda *_: zeros, pipeline_mode=pl.Buffered(1))


def _mod_kernel(c_ref, w_ref, b_ref, o_ref):
    c = c_ref[...]
    c_act = (c * jax.nn.sigmoid(c)).astype(BF16)
    o_ref[...] = _dot(c_act, w_ref[...].astype(BF16)) + b_ref[...]


def _mod_call(c_pad, w_ada, b_ada, *, tn=1024):
    rows, d = c_pad.shape
    n = w_ada.shape[1]
    return pl.pallas_call(
        _mod_kernel,
        grid=(n // tn,),
        in_specs=[
            pl.BlockSpec((rows, d), lambda j: (0, 0)),
            pl.BlockSpec((d, tn), lambda j: (0, j)),
            pl.BlockSpec((1, tn), lambda j: (0, j)),
        ],
        out_specs=pl.BlockSpec((rows, tn), lambda j: (0, j)),
        out_shape=jax.ShapeDtypeStruct((rows, n), F32),
        compiler_params=pltpu.CompilerParams(
            dimension_semantics=("arbitrary",), vmem_limit_bytes=VMEM_LIMIT),
        name="adaln_mod",
    )(c_pad, w_ada, b_ada)


def _ffn_kernel(*refs, final_norm, cast_every_step, lead, n_seed):
    seeded = n_seed > 0
    n_cast = len(cast_every_step)
    x_ref, sh_ref, sc_ref, g_ref, nw_ref, fin_ref, wg_ref, wu_ref, wd_ref = refs[:9]
    pos = 9
    seed_ref = refs[pos] if seeded else None
    pos += int(seeded)
    cast_in = refs[pos:pos + n_cast]
    pos += n_cast
    o_ref = refs[pos]
    pos += 1
    weight_out = refs[pos:pos + 3] if lead else None
    pos += 3 * int(lead)
    cast_out = refs[pos:pos + n_cast]
    n_scr = refs[pos + n_cast]
    acc_ref = o_ref
    i = pl.program_id(0)
    j = pl.program_id(1)
    last_j = pl.num_programs(1) - 1

    def cast(every_step):
        for src, dst, flag in zip(cast_in, cast_out, cast_every_step):
            if flag == every_step:
                dst[...] = src[...].astype(BF16)

    cast(True)

    @pl.when(j == 0)
    def _():
        cast(False)

    def step(first, last, n_split):
        if lead:
            wg, wu, wd = (w[...].astype(BF16) for w in (wg_ref, wu_ref, wd_ref))
            for dst, w in zip(weight_out, (wg, wu, wd)):
                dst[...] = w
        else:
            wg, wu, wd = wg_ref[...], wu_ref[...], wd_ref[...]
        sub = x_ref.shape[0] // n_split
        for r in range(n_split):
            rows = slice(r * sub, (r + 1) * sub)
            if first:
                gain = nw_ref[...] * (1.0 + sc_ref[...])
                n = (_rms_rows(x_ref[rows, :], gain) + sh_ref[...]).astype(BF16)
                n_scr[rows, :] = n
            else:
                n = n_scr[rows, :]
            y = None
            wide_enough = wg.shape[1] // FFN_K_SPLIT >= MXU_COLS
            k_split = FFN_K_SPLIT if (n_split == 1 and wide_enough) else 1
            half = wg.shape[1] // k_split
            for c in range(k_split):
                cols = slice(c * half, (c + 1) * half)
                hg = _dot(n, wg[:, cols])
                hu = _dot(n, wu[:, cols])
                hid = (hg * jax.nn.sigmoid(hg) * hu).astype(BF16)
                part = _dot(hid, wd[cols, :])
                y = part if y is None else y + part
            if not first:
                y = acc_ref[rows, :] + y
            if last:
                h = x_ref[rows, :] + (0.5 * g_ref[...]) * y
                if final_norm:
                    h = _rms_rows(h, fin_ref[...])
                o_ref[rows, :] = h
            else:
                acc_ref[rows, :] = y

    def when(cond):
        return pl.when(jnp.logical_and(cond, i >= n_seed) if seeded else cond)

    @when(j == 0)
    def _():
        step(True, False, FFN_EDGE_SPLIT)

    @when(jnp.logical_and(j > 0, j < last_j))
    def _():
        step(False, False, 1)

    @when(j == last_j)
    def _():
        step(False, True, FFN_EDGE_SPLIT)

    if seeded:
        @pl.when(jnp.logical_and(j == last_j, i < n_seed))
        def _():
            o_ref[...] = seed_ref[...]


def _ffn_call(x, mod_rows, mod_base, norm_w, fin_w, wg, wu, wd, *, seq, final_norm, tm, tf,
              cast_jobs=(), lead=False, seed=None):
    t, d = x.shape
    assert wg.shape[1] % tf == 0
    nf = wg.shape[1] // tf
    blocks_per_batch = seq // tm
    n_seed = 0 if seed is None else seed.shape[0] // tm
    seeded = n_seed > 0

    def row(k):
        return pl.BlockSpec((None, 1, d), lambda i, j: ((i // blocks_per_batch) * N_MOD + mod_base + k, 0, 0))

    def wj(i, j):
        return jnp.where(i < n_seed, 0, j) if seeded else j

    cast_arrays = [job[0] for job in cast_jobs]
    cast_specs = [pl.BlockSpec(job[1], job[2]) for job in cast_jobs]
    weight_specs = [
        pl.BlockSpec((d, tf), lambda i, j: (0, wj(i, j))),
        pl.BlockSpec((d, tf), lambda i, j: (0, wj(i, j))),
        pl.BlockSpec((tf, d), lambda i, j: (wj(i, j), 0)),
    ]
    n_rows = tm if lead else t
    outs = pl.pallas_call(
        functools.partial(_ffn_kernel, final_norm=final_norm, lead=lead, n_seed=n_seed,
                          cast_every_step=tuple(job[3] for job in cast_jobs)),
        grid=(n_rows // tm, nf),
        in_specs=[
            pl.BlockSpec((tm, d), lambda i, j: (i, 0), pipeline_mode=pl.Buffered(1 if lead else 2)),
            row(0), row(1), row(2),
            pl.BlockSpec((1, d), lambda i, j: (0, 0)),
            pl.BlockSpec((1, d), lambda i, j: (0, 0)),
            *weight_specs,
            *([pl.BlockSpec((tm, d), lambda i, j: (jnp.minimum(i, n_seed - 1), 0))] if seeded else []),
            *cast_specs,
        ],
        out_specs=[pl.BlockSpec((tm, d), lambda i, j: (i, 0)),
                   *(weight_specs if lead else []), *cast_specs],
        out_shape=[jax.ShapeDtypeStruct((n_rows, d), F32),
                   *([jax.ShapeDtypeStruct(w.shape, BF16) for w in (wg, wu, wd)] if lead else []),
                   *(jax.ShapeDtypeStruct(a.shape, BF16) for a in cast_arrays)],
        scratch_shapes=[pltpu.VMEM((tm, d), BF16)],
        compiler_params=pltpu.CompilerParams(
            dimension_semantics=("parallel", "arbitrary"), vmem_limit_bytes=VMEM_LIMIT),
        name="swiglu_ffn_lead" if lead else "swiglu_ffn",
    )(x, mod_rows, mod_rows, mod_rows, norm_w, fin_w, wg, wu, wd,
      *([seed] if seeded else []), *cast_arrays)
    return outs[0], outs[1:]


def _mix_kernel(h_ref, sh_ref, sc_ref, nw_ref, win_ref, wkpe_ref, vn_ref, ws_ref, bs_ref,
                qln_ref, wuq_ref, kvn_ref, wk_ref, wvt_ref, qng_ref, qrg_ref, kng_ref, krg_ref,
                ong_ref, pos_ref, freq_ref,
                a_ref, q_ref, k_ref, vt_ref,
                u_scr, v_scr, a_scr, *, tm):
    o1 = GMLP_WIDTH
    o2 = 2 * GMLP_WIDTH
    o3 = o2 + Q_RANK
    o4 = o3 + KV_RANK
    scale = (QK_NOPE + QK_ROPE) ** -0.5 * LOG2_E
    sub = tm // MIX_SPLIT

    for r in range(MIX_SPLIT):
        tok = slice(r * sub, (r + 1) * sub)
        gain = nw_ref[...] * (1.0 + sc_ref[...])
        n = (_rms_rows(h_ref[tok, :], gain) + sh_ref[...]).astype(BF16)

        u_scr[tok, :] = _gelu_tanh(_dot_nt(n, win_ref[:o1, :]))
        vv = _gelu_tanh(_dot_nt(n, win_ref[o1:o2, :]))
        v_scr[tok, :] = _rms_rows(vv, vn_ref[...]).astype(BF16)
        for c in range(r * sub // CHUNK, (r + 1) * sub // CHUNK):
            rows = slice(c * CHUNK, (c + 1) * CHUNK)
            for g in range(GMLP_HEADS):
                cols = slice(g * GMLP_GROUP, (g + 1) * GMLP_GROUP)
                mixed = _dot(ws_ref[g], v_scr[rows, cols]) + bs_ref[:, cols]
                a_scr[rows, cols] = u_scr[rows, cols] * mixed
        a_ref[tok, :] = _rms_rows(a_scr[tok, :], ong_ref[...]).astype(BF16)

        ang = pos_ref[tok, :].astype(F32) * freq_ref[...]
        lane = lax.broadcasted_iota(jnp.int32, ang.shape, 1)
        sin = jnp.sin(ang)
        rot = jnp.where(lane < QK_ROPE, jnp.cos(ang), jnp.where(lane < QK_ROPE + QK_ROPE // 2, -sin, sin))

        def rope(z, gain):
            w = _rms_rows(z, gain) * rot
            return w + pltpu.roll(w, QK_ROPE, axis=1)

        q_lat = _rms_rows(_dot_nt(n, win_ref[o2:o3, :]), qln_ref[...]).astype(BF16)
        q_nope_gain = qng_ref[...] * scale
        q_rope_gain = qrg_ref[...] * scale
        for hd in range(MLA_HEADS):
            base = hd * QK_PAD
            qq = _dot(q_lat, wuq_ref[:, base:base + QK_PAD])
            q_ref[tok, base:base + LANES] = _rms_rows(qq[:, :LANES], q_nope_gain).astype(BF16)
            q_ref[tok, base + LANES:base + QK_PAD] = rope(qq[:, LANES:], q_rope_gain).astype(BF16)

        k_pe = rope(_dot_nt(n, wkpe_ref[...]), krg_ref[...])
        k_pe = jnp.where(lane < QK_ROPE, k_pe, 0.0).astype(BF16)
        kv_lat = _rms_rows(_dot_nt(n, win_ref[o3:o4, :]), kvn_ref[...]).astype(BF16)
        k_nope = _dot(kv_lat, wk_ref[...])
        for hd in range(MLA_HEADS):
            base = hd * QK_PAD
            cols = slice(hd * QK_NOPE, (hd + 1) * QK_NOPE)
            k_ref[tok, base:base + LANES] = _rms_rows(k_nope[:, cols], kng_ref[...]).astype(BF16)
            k_ref[tok, base + LANES:base + QK_PAD] = k_pe
        vt_ref[:, tok] = _dot_nt(wvt_ref[...], kv_lat).astype(BF16)


def _mix_call(h, mod_rows, mix_norm, win, wkpe, v_norm, ws, bs_full, q_lat_norm, wuq, kv_lat_norm, wk, wvt,
              qn_g, qr_g, kn_g, kr_g, on_g, pos, freq, *, seq, tm=512):
    t, d = h.shape
    blocks_per_batch = seq // tm
    batch = t // seq

    def row(k):
        return pl.BlockSpec((None, 1, d), lambda i: ((i // blocks_per_batch) * N_MOD + 3 + k, 0, 0))

    def tok(width):
        return pl.BlockSpec((tm, width), lambda i: (i, 0))

    return pl.pallas_call(
        functools.partial(_mix_kernel, tm=tm),
        grid=(t // tm,),
        in_specs=[
            tok(d), row(0), row(1),
            _const_spec(mix_norm.shape), _const_spec(win.shape), _const_spec(wkpe.shape),
            _const_spec(v_norm.shape),
            _const_spec(ws.shape), _const_spec(bs_full.shape),
            _const_spec(q_lat_norm.shape), _const_spec(wuq.shape),
            _const_spec(kv_lat_norm.shape), _const_spec(wk.shape), _const_spec(wvt.shape),
            _const_spec(qn_g.shape), _const_spec(qr_g.shape), _const_spec(kn_g.shape),
            _const_spec(kr_g.shape), _const_spec(on_g.shape),
            tok(1), _const_spec(freq.shape),
        ],
        out_specs=[
            tok(GMLP_WIDTH), tok(MLA_HEADS * QK_PAD), tok(MLA_HEADS * QK_PAD),
            pl.BlockSpec((MLA_WIDTH, tm), lambda i: (i // blocks_per_batch, i % blocks_per_batch)),
        ],
        out_shape=[
            jax.ShapeDtypeStruct((t, GMLP_WIDTH), BF16),
            jax.ShapeDtypeStruct((t, MLA_HEADS * QK_PAD), BF16),
            jax.ShapeDtypeStruct((t, MLA_HEADS * QK_PAD), BF16),
            jax.ShapeDtypeStruct((batch * MLA_WIDTH, seq), BF16),
        ],
        scratch_shapes=[
            pltpu.VMEM((tm, GMLP_WIDTH), F32),
            pltpu.VMEM((tm, GMLP_WIDTH), BF16),
            pltpu.VMEM((tm, GMLP_WIDTH), F32),
        ],
        compiler_params=pltpu.CompilerParams(
            dimension_semantics=("parallel",), vmem_limit_bytes=VMEM_LIMIT),
        name="mixer_in",
    )(h, mod_rows, mod_rows, mix_norm, win, wkpe, v_norm, ws, bs_full, q_lat_norm, wuq, kv_lat_norm, wk, wvt,
      qn_g, qr_g, kn_g, kr_g, on_g, pos, freq)


def _attn_kernel(q_ref, k_ref, vt_ref, o_ref, vx_scr, s_scr, p_scr, *, sub, nsub):
    vx_scr[:V_HEAD, :] = vt_ref[...]
    vx_scr[V_HEAD:, :] = jnp.ones((vx_scr.shape[0] - V_HEAD, vx_scr.shape[1]), BF16)

    def rows(i):
        return slice(i * sub, (i + 1) * sub)

    def qk(i):
        s_scr[i % ATTN_SLOTS] = _dot_nt(k_ref[...], q_ref[rows(i), :])

    def softmax(i):
        s = s_scr[i % ATTN_SLOTS]
        p_scr[i % ATTN_SLOTS] = jnp.exp2(s - jnp.max(s, axis=0, keepdims=True)).astype(BF16)

    def pv(i):
        ox = _dot(vx_scr[...], p_scr[i % ATTN_SLOTS])
        o_ref[rows(i), :] = (ox[:V_HEAD, :] / ox[V_HEAD:V_HEAD + 1, :]).T

    qk(0)
    softmax(0)
    qk(1)
    for i in range(nsub - 2):
        qk(i + 2)
        softmax(i + 1)
        pv(i)
    softmax(nsub - 1)
    pv(nsub - 2)
    pv(nsub - 1)


def _attn_call(q, k, vt, *, batch, seq, sub=ATTN_SUB):
    t = q.shape[0]
    nsub = seq // sub
    assert nsub >= 3
    return pl.pallas_call(
        functools.partial(_attn_kernel, sub=sub, nsub=nsub),
        grid=(batch, MLA_HEADS),
        in_specs=[
            pl.BlockSpec((seq, QK_PAD), lambda b, h: (b, h)),
            pl.BlockSpec((seq, QK_PAD), lambda b, h: (b, h)),
            pl.BlockSpec((V_HEAD, seq), lambda b, h: (b * MLA_HEADS + h, 0)),
        ],
        out_specs=pl.BlockSpec((seq, V_HEAD), lambda b, h: (b, h)),
        out_shape=jax.ShapeDtypeStruct((t, MLA_WIDTH), F32),
        scratch_shapes=[
            pltpu.VMEM((V_HEAD + BF16_SUBLANES, seq), BF16),
            pltpu.VMEM((ATTN_SLOTS, seq, sub), F32),
            pltpu.VMEM((ATTN_SLOTS, seq, sub), BF16),
        ],
        compiler_params=pltpu.CompilerParams(
            dimension_semantics=("parallel", "parallel"), vmem_limit_bytes=VMEM_LIMIT),
        name="mla_attention",
    )(q, k, vt)


def _outproj_kernel(h_ref, a_ref, m_ref, g_ref, mg_ref, wo_ref, *rest):
    n_cast = (len(rest) - 1) // 2
    cast_in, o_ref, cast_out = rest[:n_cast], rest[n_cast], rest[n_cast + 1:]
    for src, dst in zip(cast_in, cast_out):
        dst[...] = src[...].astype(BF16)
    m_n = _rms_rows(m_ref[...], mg_ref[...]).astype(BF16)
    y = _dot(a_ref[...], wo_ref[:GMLP_WIDTH, :]) + _dot(m_n, wo_ref[GMLP_WIDTH:, :])
    o_ref[...] = h_ref[...] + g_ref[...] * y


def _outproj_call(h, a, m, mod_rows, mla_gain, wo, cast_arrays, *, seq, tm=512):
    t, d = h.shape
    blocks_per_batch = seq // tm
    steps = t // tm
    cast_specs = [pl.BlockSpec((w.shape[0] // steps, w.shape[1]), lambda i: (i, 0)) for w in cast_arrays]
    outs = pl.pallas_call(
        _outproj_kernel,
        grid=(steps,),
        in_specs=[
            pl.BlockSpec((tm, d), lambda i: (i, 0)),
            pl.BlockSpec((tm, GMLP_WIDTH), lambda i: (i, 0)),
            pl.BlockSpec((tm, MLA_WIDTH), lambda i: (i, 0)),
            pl.BlockSpec((None, 1, d), lambda i: ((i // blocks_per_batch) * N_MOD + 5, 0, 0)),
            _const_spec(mla_gain.shape), _const_spec(wo.shape),
            *cast_specs,
        ],
        out_specs=[pl.BlockSpec((tm, d), lambda i: (i, 0)), *cast_specs],
        out_shape=[jax.ShapeDtypeStruct((t, d), F32),
                   *(jax.ShapeDtypeStruct(w.shape, BF16) for w in cast_arrays)],
        compiler_params=pltpu.CompilerParams(
            dimension_semantics=("parallel",), vmem_limit_bytes=VMEM_LIMIT),
        name="mixer_out",
    )(h, a, m, mod_rows, mla_gain, wo, *cast_arrays)
    return outs[0], outs[1:]


def _half_swap(x):
    half = x.shape[-1] // 2
    return jnp.concatenate([x[..., half:], x[..., :half]], axis=-1)


def kernel(x, c, positions, w_ada, b_ada, ffn1_norm, ffn1_w_gate, ffn1_w_up, ffn1_w_down, mix_norm, w_in, gmlp_v_norm, gmlp_w_s, gmlp_b_s, q_lat_norm, w_uq, kv_lat_norm, w_ukv, q_nope_norm, q_rope_norm, k_nope_norm, k_rope_norm, out_norm_gmlp, out_norm_mla, w_out, ffn2_norm, ffn2_w_gate, ffn2_w_up, ffn2_w_down, final_norm):
    batch, seq, d = x.shape
    depth = w_ada.shape[0]
    t = batch * seq

    inv_freq = ROPE_THETA ** (-jnp.arange(0, QK_ROPE, 2, dtype=F32) / QK_ROPE)
    freq = jnp.tile(inv_freq, 4)[None, :]
    pos = positions.reshape(t, 1)
    c_pad = jnp.pad(c, ((0, 8 - batch), (0, 0)))

    def row(v):
        return v[None, :]

    h = x.reshape(t, d)
    for l in range(depth):
        mod = _mod_call(c_pad, w_ada[l], row(b_ada[l]))
        mod_rows = mod[:batch].reshape(batch * N_MOD, 1, d)

        w_in_t = w_in[l].T
        ni = t // FFN_TM
        jobs = [
            (ffn2_w_down[l], (FFN_TF, d // ni), lambda i, j: (j, i), True),
            (w_in_t, (IN_COLS, d // ni), lambda i, j: (0, i), False),
            (w_out[l], (d // ni, d), lambda i, j: (i, 0), False),
        ]
        h_lead, ffn1_weights = _ffn_call(
            h, mod_rows, 0, row(ffn1_norm[l]), row(final_norm[l]),
            ffn1_w_gate[l], ffn1_w_up[l], ffn1_w_down[l],
            seq=seq, final_norm=False, tm=FFN_LEAD_TM, tf=FFN_LEAD_TF, lead=True)
        h, (wd2, win, wout) = _ffn_call(
            h, mod_rows, 0, row(ffn1_norm[l]), row(final_norm[l]), *ffn1_weights,
            seq=seq, final_norm=False, cast_jobs=jobs, seed=h_lead, tm=FFN_TM, tf=FFN_TF)

        k_rope_w = win[IN_COLS - QK_ROPE:]
        half = QK_ROPE // 2
        wkpe = jnp.concatenate([k_rope_w, k_rope_w[half:], k_rope_w[:half]], axis=0)
        uq = w_uq[l].reshape(Q_RANK, MLA_HEADS, QK_NOPE + QK_ROPE)
        uq_rope = uq[..., QK_NOPE:]
        wuq = jnp.concatenate([uq[..., :QK_NOPE], uq_rope, _half_swap(uq_rope)], axis=-1)
        wuq = wuq.reshape(Q_RANK, MLA_HEADS * QK_PAD).astype(BF16)
        bs_full = jnp.repeat(gmlp_b_s[l].T, GMLP_GROUP, axis=1)
        ukv = w_ukv[l].reshape(KV_RANK, MLA_HEADS, QK_NOPE + V_HEAD)
        wk = ukv[..., :QK_NOPE].reshape(KV_RANK, MLA_HEADS * QK_NOPE).astype(BF16)
        wvt = ukv[..., QK_NOPE:].reshape(KV_RANK, MLA_WIDTH).T.astype(BF16)
        a, q, k, vt = _mix_call(
            h, mod_rows, row(mix_norm[l]), win, wkpe, row(gmlp_v_norm[l]), gmlp_w_s[l].astype(BF16), bs_full,
            row(q_lat_norm[l]), wuq, row(kv_lat_norm[l]), wk, wvt,
            row(q_nope_norm[l]), row(jnp.concatenate([q_rope_norm[l], _half_swap(q_rope_norm[l])])),
            row(k_nope_norm[l]), row(jnp.concatenate([k_rope_norm[l], _half_swap(k_rope_norm[l])])),
            row(out_norm_gmlp[l]), pos, freq, seq=seq)

        m = _attn_call(q, k, vt, batch=batch, seq=seq)
        h, (wg2, wu2) = _outproj_call(
            h, a, m, mod_rows, row(out_norm_mla[l]), wout, [ffn2_w_gate[l], ffn2_w_up[l]], seq=seq)

        h, _ = _ffn_call(h, mod_rows, 6, row(ffn2_norm[l]), row(final_norm[l]), wg2, wu2, wd2,
                         seq=seq, final_norm=True, tm=FFN_TM, tf=FFN_TF)
    return h.reshape(batch, seq, d)
```

```python
import functools
import math

import jax
import jax.numpy as jnp
from jax import lax
from jax.experimental import pallas as pl
from jax.experimental.pallas import tpu as pltpu

D_MODEL = 2048
GMLP_WIDTH = 1024
GMLP_GROUP = 128
GMLP_HEADS = GMLP_WIDTH // GMLP_GROUP
CHUNK = 128
V_HEAD = 128
MLA_WIDTH = 1024
MLA_HEADS = MLA_WIDTH // V_HEAD
QK_NOPE = 128
QK_ROPE = 64
Q_RANK = D_MODEL // 4
KV_RANK = D_MODEL // 8
IN_COLS = 2 * GMLP_WIDTH + Q_RANK + KV_RANK + QK_ROPE
D_FF = 5632
ROPE_THETA = 10000.0
EPS = 1e-6
N_MOD = 9
LOG2_E = 1.4426950408889634

LANES = 128
MXU_COLS = 256
QK_PAD = 2 * LANES
VMEM_LIMIT = 56 * 1024 * 1024
BF16_SUBLANES = 16
ATTN_SLOTS = 2
ATTN_SUB = 512
MIX_TM = 512
FFN_TM = 512
FFN_TF = 512
FFN_LEAD_TF = 256
FFN_LEAD_TM = 1024
FFN_K_SPLIT = 2
FFN_EDGE_SPLIT = 2

F32 = jnp.float32
BF16 = jnp.bfloat16


def _dot(a, b):
    return jnp.dot(a, b, preferred_element_type=F32)


def _dot_nt(a, b):
    return lax.dot_general(a, b, (((1,), (1,)), ((), ())), preferred_element_type=F32)


def _gelu_tanh(x):
    c0 = math.sqrt(2.0 / math.pi)
    hx = 0.5 * x
    return hx + hx * jnp.tanh(x * (c0 + (0.044715 * c0) * (x * x)))


def _rms_rows(x, gain):
    return x * lax.rsqrt(jnp.mean(x * x, axis=-1, keepdims=True) + EPS) * gain


def _const_spec(shape):
    zeros = (0,) * len(shape)
    return pl.BlockSpec(shape, lambda *_: zeros, pipeline_mode=pl.Buffered(1))


def _mod_kernel(c_ref, w_ref, b_ref, o_ref):
    c = c_ref[...]
    c_act = (c * jax.nn.sigmoid(c)).astype(BF16)
    o_ref[...] = _dot(c_act, w_ref[...].astype(BF16)) + b_ref[...]


def _mod_call(c_pad, w_ada, b_ada, *, tn=1024):
    rows, d = c_pad.shape
    n = w_ada.shape[1]
    return pl.pallas_call(
        _mod_kernel,
        grid=(n // tn,),
        in_specs=[
            pl.BlockSpec((rows, d), lambda j: (0, 0)),
            pl.BlockSpec((d, tn), lambda j: (0, j)),
            pl.BlockSpec((1, tn), lambda j: (0, j)),
        ],
        out_specs=pl.BlockSpec((rows, tn), lambda j: (0, j)),
        out_shape=jax.ShapeDtypeStruct((rows, n), F32),
        compiler_params=pltpu.CompilerParams(
            dimension_semantics=("arbitrary",), vmem_limit_bytes=VMEM_LIMIT),
        name="adaln_mod",
    )(c_pad, w_ada, b_ada)


def _ffn_kernel(*refs, final_norm, cast_every_step, lead, n_seed):
    seeded = n_seed > 0
    n_cast = len(cast_every_step)
    x_ref, sh_ref, sc_ref, g_ref, nw_ref, fin_ref, wg_ref, wu_ref, wd_ref = refs[:9]
    pos = 9
    seed_ref = refs[pos] if seeded else None
    pos += int(seeded)
    cast_in = refs[pos:pos + n_cast]
    pos += n_cast
    o_ref = refs[pos]
    pos += 1
    weight_out = refs[pos:pos + 3] if lead else None
    pos += 3 * int(lead)
    cast_out = refs[pos:pos + n_cast]
    n_scr, *acc = refs[pos + n_cast:]
    acc_ref = o_ref if lead else acc[0]
    i = pl.program_id(0)
    j = pl.program_id(1)
    last_j = pl.num_programs(1) - 1

    def cast(every_step):
        for src, dst, flag in zip(cast_in, cast_out, cast_every_step):
            if flag == every_step:
                dst[...] = src[...].astype(BF16)

    cast(True)

    @pl.when(j == 0)
    def _():
        cast(False)

    def step(first, last, n_split):
        if lead:
            wg, wu, wd = (w[...].astype(BF16) for w in (wg_ref, wu_ref, wd_ref))
            for dst, w in zip(weight_out, (wg, wu, wd)):
                dst[...] = w
        else:
            wg, wu, wd = wg_ref[...], wu_ref[...], wd_ref[...]
        sub = x_ref.shape[0] // n_split
        for r in range(n_split):
            rows = slice(r * sub, (r + 1) * sub)
            if first:
                gain = nw_ref[...] * (1.0 + sc_ref[...])
                n = (_rms_rows(x_ref[rows, :], gain) + sh_ref[...]).astype(BF16)
                n_scr[rows, :] = n
            else:
                n = n_scr[rows, :]
            y = None
            wide_enough = wg.shape[1] // FFN_K_SPLIT >= MXU_COLS
            k_split = FFN_K_SPLIT if (n_split == 1 and wide_enough) else 1
            half = wg.shape[1] // k_split
            for c in range(k_split):
                cols = slice(c * half, (c + 1) * half)
                hg = _dot(n, wg[:, cols])
                hu = _dot(n, wu[:, cols])
                hid = (hg * jax.nn.sigmoid(hg) * hu).astype(BF16)
                part = _dot(hid, wd[cols, :])
                y = part if y is None else y + part
            if not first:
                y = acc_ref[rows, :] + y
            if last:
                h = x_ref[rows, :] + (0.5 * g_ref[...]) * y
                if final_norm:
                    h = _rms_rows(h, fin_ref[...])
                o_ref[rows, :] = h
            else:
                acc_ref[rows, :] = y

    def when(cond):
        return pl.when(jnp.logical_and(cond, i >= n_seed) if seeded else cond)

    @when(j == 0)
    def _():
        step(True, False, FFN_EDGE_SPLIT)

    @when(jnp.logical_and(j > 0, j < last_j))
    def _():
        step(False, False, 1)

    @when(j == last_j)
    def _():
        step(False, True, FFN_EDGE_SPLIT)

    if seeded:
        @pl.when(jnp.logical_and(j == last_j, i < n_seed))
        def _():
            o_ref[...] = seed_ref[...]


def _ffn_call(x, mod_rows, mod_base, norm_w, fin_w, wg, wu, wd, *, seq, final_norm, tm, tf,
              cast_jobs=(), lead=False, seed=None):
    t, d = x.shape
    nf = wg.shape[1] // tf
    blocks_per_batch = seq // tm
    n_seed = 0 if seed is None else seed.shape[0] // tm
    seeded = n_seed > 0

    def row(k):
        return pl.BlockSpec((None, 1, d), lambda i, j: ((i // blocks_per_batch) * N_MOD + mod_base + k, 0, 0))

    def wj(i, j):
        return jnp.where(i < n_seed, 0, j) if seeded else j

    cast_arrays = [job[0] for job in cast_jobs]
    cast_specs = [pl.BlockSpec(job[1], job[2]) for job in cast_jobs]
    weight_specs = [
        pl.BlockSpec((d, tf), lambda i, j: (0, wj(i, j))),
        pl.BlockSpec((d, tf), lambda i, j: (0, wj(i, j))),
        pl.BlockSpec((tf, d), lambda i, j: (wj(i, j), 0)),
    ]
    n_rows = tm if lead else t
    outs = pl.pallas_call(
        functools.partial(_ffn_kernel, final_norm=final_norm, lead=lead, n_seed=n_seed,
                          cast_every_step=tuple(job[3] for job in cast_jobs)),
        grid=(n_rows // tm, nf),
        in_specs=[
            pl.BlockSpec((tm, d), lambda i, j: (i, 0), pipeline_mode=pl.Buffered(1 if lead else 2)),
            row(0), row(1), row(2),
            pl.BlockSpec((1, d), lambda i, j: (0, 0)),
            pl.BlockSpec((1, d), lambda i, j: (0, 0)),
            *weight_specs,
            *([pl.BlockSpec((tm, d), lambda i, j: (jnp.minimum(i, n_seed - 1), 0))] if seeded else []),
            *cast_specs,
        ],
        out_specs=[pl.BlockSpec((tm, d), lambda i, j: (i, 0)),
                   *(weight_specs if lead else []), *cast_specs],
        out_shape=[jax.ShapeDtypeStruct((n_rows, d), F32),
                   *([jax.ShapeDtypeStruct(w.shape, BF16) for w in (wg, wu, wd)] if lead else []),
                   *(jax.ShapeDtypeStruct(a.shape, BF16) for a in cast_arrays)],
        scratch_shapes=[pltpu.VMEM((tm, d), BF16), *([] if lead else [pltpu.VMEM((tm, d), F32)])],
        compiler_params=pltpu.CompilerParams(
            dimension_semantics=("parallel", "arbitrary"), vmem_limit_bytes=VMEM_LIMIT),
        name="swiglu_ffn_lead" if lead else "swiglu_ffn",
    )(x, mod_rows, mod_rows, mod_rows, norm_w, fin_w, wg, wu, wd,
      *([seed] if seeded else []), *cast_arrays)
    return outs[0], outs[1:]


def _mix_kernel(h_ref, sh_ref, sc_ref, nw_ref, win_ref, wkpe_ref, vn_ref, ws_ref, bs_ref,
                qln_ref, wuq_ref, kvn_ref, wk_ref, wvt_ref, qng_ref, qrg_ref, kng_ref, krg_ref,
                ong_ref, pos_ref, freq_ref,
                a_ref, q_ref, k_ref, vt_ref,
                u_scr, v_scr, a_scr, *, tm):
    o1 = GMLP_WIDTH
    o2 = 2 * GMLP_WIDTH
    o3 = o2 + Q_RANK
    o4 = o3 + KV_RANK
    scale = (QK_NOPE + QK_ROPE) ** -0.5 * LOG2_E

    gain = nw_ref[...] * (1.0 + sc_ref[...])
    n = (_rms_rows(h_ref[...], gain) + sh_ref[...]).astype(BF16)

    u_scr[...] = _gelu_tanh(_dot_nt(n, win_ref[:o1, :]))
    vv = _gelu_tanh(_dot_nt(n, win_ref[o1:o2, :]))
    v_scr[...] = _rms_rows(vv, vn_ref[...]).astype(BF16)
    for c in range(tm // CHUNK):
        rows = slice(c * CHUNK, (c + 1) * CHUNK)
        for g in range(GMLP_HEADS):
            cols = slice(g * GMLP_GROUP, (g + 1) * GMLP_GROUP)
            mixed = _dot(ws_ref[g], v_scr[rows, cols]) + bs_ref[:, cols]
            a_scr[rows, cols] = u_scr[rows, cols] * mixed
    a_ref[...] = _rms_rows(a_scr[...], ong_ref[...]).astype(BF16)

    ang = pos_ref[...].astype(F32) * freq_ref[...]
    lane = lax.broadcasted_iota(jnp.int32, ang.shape, 1)
    sin = jnp.sin(ang)
    rot = jnp.where(lane < QK_ROPE, jnp.cos(ang), jnp.where(lane < QK_ROPE + QK_ROPE // 2, -sin, sin))

    def rope(z, gain):
        w = _rms_rows(z, gain) * rot
        return w + pltpu.roll(w, QK_ROPE, axis=1)

    q_lat = _rms_rows(_dot_nt(n, win_ref[o2:o3, :]), qln_ref[...]).astype(BF16)
    q_nope_gain = qng_ref[...] * scale
    q_rope_gain = qrg_ref[...] * scale
    for hd in range(MLA_HEADS):
        base = hd * QK_PAD
        qq = _dot(q_lat, wuq_ref[:, base:base + QK_PAD])
        q_ref[:, base:base + LANES] = _rms_rows(qq[:, :LANES], q_nope_gain).astype(BF16)
        q_ref[:, base + LANES:base + QK_PAD] = rope(qq[:, LANES:], q_rope_gain).astype(BF16)

    k_pe = rope(_dot_nt(n, wkpe_ref[...]), krg_ref[...])
    k_pe = jnp.where(lane < QK_ROPE, k_pe, 0.0).astype(BF16)
    kv_lat = _rms_rows(_dot_nt(n, win_ref[o3:o4, :]), kvn_ref[...]).astype(BF16)
    k_nope = _dot(kv_lat, wk_ref[...])
    for hd in range(MLA_HEADS):
        base = hd * QK_PAD
        cols = slice(hd * QK_NOPE, (hd + 1) * QK_NOPE)
        k_ref[:, base:base + LANES] = _rms_rows(k_nope[:, cols], kng_ref[...]).astype(BF16)
        k_ref[:, base + LANES:base + QK_PAD] = k_pe
    vt_ref[...] = _dot_nt(wvt_ref[...], kv_lat).astype(BF16)


def _mix_call(h, mod_rows, mix_norm, win, wkpe, v_norm, ws, bs_full, q_lat_norm, wuq, kv_lat_norm, wk, wvt,
              qn_g, qr_g, kn_g, kr_g, on_g, pos, freq, *, seq, tm=MIX_TM):
    t, d = h.shape
    blocks_per_batch = seq // tm
    batch = t // seq

    def row(k):
        return pl.BlockSpec((None, 1, d), lambda i: ((i // blocks_per_batch) * N_MOD + 3 + k, 0, 0))

    def tok(width):
        return pl.BlockSpec((tm, width), lambda i: (i, 0))

    return pl.pallas_call(
        functools.partial(_mix_kernel, tm=tm),
        grid=(t // tm,),
        in_specs=[
            tok(d), row(0), row(1),
            _const_spec(mix_norm.shape), _const_spec(win.shape), _const_spec(wkpe.shape),
            _const_spec(v_norm.shape),
            _const_spec(ws.shape), _const_spec(bs_full.shape),
            _const_spec(q_lat_norm.shape), _const_spec(wuq.shape),
            _const_spec(kv_lat_norm.shape), _const_spec(wk.shape), _const_spec(wvt.shape),
            _const_spec(qn_g.shape), _const_spec(qr_g.shape), _const_spec(kn_g.shape),
            _const_spec(kr_g.shape), _const_spec(on_g.shape),
            tok(1), _const_spec(freq.shape),
        ],
        out_specs=[
            tok(GMLP_WIDTH), tok(MLA_HEADS * QK_PAD), tok(MLA_HEADS * QK_PAD),
            pl.BlockSpec((MLA_WIDTH, tm), lambda i: (i // blocks_per_batch, i % blocks_per_batch)),
        ],
        out_shape=[
            jax.ShapeDtypeStruct((t, GMLP_WIDTH), BF16),
            jax.ShapeDtypeStruct((t, MLA_HEADS * QK_PAD), BF16),
            jax.ShapeDtypeStruct((t, MLA_HEADS * QK_PAD), BF16),
            jax.ShapeDtypeStruct((batch * MLA_WIDTH, seq), BF16),
        ],
        scratch_shapes=[
            pltpu.VMEM((tm, GMLP_WIDTH), F32),
            pltpu.VMEM((tm, GMLP_WIDTH), BF16),
            pltpu.VMEM((tm, GMLP_WIDTH), F32),
        ],
        compiler_params=pltpu.CompilerParams(
            dimension_semantics=("parallel",), vmem_limit_bytes=VMEM_LIMIT),
        name="mixer_in",
    )(h, mod_rows, mod_rows, mix_norm, win, wkpe, v_norm, ws, bs_full, q_lat_norm, wuq, kv_lat_norm, wk, wvt,
      qn_g, qr_g, kn_g, kr_g, on_g, pos, freq)


def _attn_kernel(q_ref, k_ref, vt_ref, o_ref, vx_scr, s_scr, p_scr, *, sub, nsub):
    vx_scr[:V_HEAD, :] = vt_ref[...]
    vx_scr[V_HEAD:, :] = jnp.ones((vx_scr.shape[0] - V_HEAD, vx_scr.shape[1]), BF16)

    def rows(i):
        return slice(i * sub, (i + 1) * sub)

    def qk(i):
        s_scr[i % ATTN_SLOTS] = _dot_nt(k_ref[...], q_ref[rows(i), :])

    def softmax(i):
        s = s_scr[i % ATTN_SLOTS]
        p_scr[i % ATTN_SLOTS] = jnp.exp2(s - jnp.max(s, axis=0, keepdims=True)).astype(BF16)

    def pv(i):
        ox = _dot(vx_scr[...], p_scr[i % ATTN_SLOTS])
        o_ref[rows(i), :] = (ox[:V_HEAD, :] / ox[V_HEAD:V_HEAD + 1, :]).T

    qk(0)
    softmax(0)
    qk(1)
    for i in range(nsub - 2):
        pv(i)
        softmax(i + 1)
        qk(i + 2)
    pv(nsub - 2)
    softmax(nsub - 1)
    pv(nsub - 1)


def _attn_call(q, k, vt, *, batch, seq, sub=ATTN_SUB):
    t = q.shape[0]
    nsub = seq // sub
    assert nsub >= 3
    return pl.pallas_call(
        functools.partial(_attn_kernel, sub=sub, nsub=nsub),
        grid=(batch, MLA_HEADS),
        in_specs=[
            pl.BlockSpec((seq, QK_PAD), lambda b, h: (b, h)),
            pl.BlockSpec((seq, QK_PAD), lambda b, h: (b, h)),
            pl.BlockSpec((V_HEAD, seq), lambda b, h: (b * MLA_HEADS + h, 0)),
        ],
        out_specs=pl.BlockSpec((seq, V_HEAD), lambda b, h: (b, h)),
        out_shape=jax.ShapeDtypeStruct((t, MLA_WIDTH), F32),
        scratch_shapes=[
            pltpu.VMEM((V_HEAD + BF16_SUBLANES, seq), BF16),
            pltpu.VMEM((ATTN_SLOTS, seq, sub), F32),
            pltpu.VMEM((ATTN_SLOTS, seq, sub), BF16),
        ],
        compiler_params=pltpu.CompilerParams(
            dimension_semantics=("parallel", "parallel"), vmem_limit_bytes=VMEM_LIMIT),
        name="mla_attention",
    )(q, k, vt)


def _outproj_kernel(h_ref, a_ref, m_ref, g_ref, mg_ref, wo_ref, o_ref):
    m_n = _rms_rows(m_ref[...], mg_ref[...]).astype(BF16)
    y = _dot(a_ref[...], wo_ref[:GMLP_WIDTH, :]) + _dot(m_n, wo_ref[GMLP_WIDTH:, :])
    o_ref[...] = h_ref[...] + g_ref[...] * y


def _outproj_call(h, a, m, mod_rows, mla_gain, wo, *, seq, tm=MIX_TM):
    t, d = h.shape
    blocks_per_batch = seq // tm
    return pl.pallas_call(
        _outproj_kernel,
        grid=(t // tm,),
        in_specs=[
            pl.BlockSpec((tm, d), lambda i: (i, 0)),
            pl.BlockSpec((tm, GMLP_WIDTH), lambda i: (i, 0)),
            pl.BlockSpec((tm, MLA_WIDTH), lambda i: (i, 0)),
            pl.BlockSpec((None, 1, d), lambda i: ((i // blocks_per_batch) * N_MOD + 5, 0, 0)),
            _const_spec(mla_gain.shape), _const_spec(wo.shape),
        ],
        out_specs=pl.BlockSpec((tm, d), lambda i: (i, 0)),
        out_shape=jax.ShapeDtypeStruct((t, d), F32),
        compiler_params=pltpu.CompilerParams(
            dimension_semantics=("parallel",), vmem_limit_bytes=VMEM_LIMIT),
        name="mixer_out",
    )(h, a, m, mod_rows, mla_gain, wo)


def _half_swap(x):
    half = x.shape[-1] // 2
    return jnp.concatenate([x[..., half:], x[..., :half]], axis=-1)


def kernel(x, c, positions, w_ada, b_ada, ffn1_norm, ffn1_w_gate, ffn1_w_up, ffn1_w_down, mix_norm, w_in, gmlp_v_norm, gmlp_w_s, gmlp_b_s, q_lat_norm, w_uq, kv_lat_norm, w_ukv, q_nope_norm, q_rope_norm, k_nope_norm, k_rope_norm, out_norm_gmlp, out_norm_mla, w_out, ffn2_norm, ffn2_w_gate, ffn2_w_up, ffn2_w_down, final_norm):
    batch, seq, d = x.shape
    depth = w_ada.shape[0]
    t = batch * seq

    inv_freq = ROPE_THETA ** (-jnp.arange(0, QK_ROPE, 2, dtype=F32) / QK_ROPE)
    freq = jnp.tile(inv_freq, 4)[None, :]
    pos = positions.reshape(t, 1)
    c_pad = jnp.pad(c, ((0, 8 - batch), (0, 0)))

    def row(v):
        return v[None, :]

    h = x.reshape(t, d)
    for l in range(depth):
        mod = _mod_call(c_pad, w_ada[l], row(b_ada[l]))
        mod_rows = mod[:batch].reshape(batch * N_MOD, 1, d)

        w_in_t = w_in[l].T
        ni = t // FFN_TM
        jobs = [
            (ffn2_w_gate[l], (d // ni, FFN_TF), lambda i, j: (i, j), True),
            (ffn2_w_up[l], (d // ni, FFN_TF), lambda i, j: (i, j), True),
            (ffn2_w_down[l], (FFN_TF, d // ni), lambda i, j: (j, i), True),
            (w_in_t, (IN_COLS, d // ni), lambda i, j: (0, i), False),
            (w_out[l], (d // ni, d), lambda i, j: (i, 0), False),
        ]
        h_lead, ffn1_weights = _ffn_call(
            h, mod_rows, 0, row(ffn1_norm[l]), row(final_norm[l]),
            ffn1_w_gate[l], ffn1_w_up[l], ffn1_w_down[l],
            seq=seq, final_norm=False, tm=FFN_LEAD_TM, tf=FFN_LEAD_TF, lead=True)
        h, (wg2, wu2, wd2, win, wout) = _ffn_call(
            h, mod_rows, 0, row(ffn1_norm[l]), row(final_norm[l]), *ffn1_weights,
            seq=seq, final_norm=False, cast_jobs=jobs, seed=h_lead, tm=FFN_TM, tf=FFN_TF)

        k_rope_w = win[IN_COLS - QK_ROPE:]
        half = QK_ROPE // 2
        wkpe = jnp.concatenate([k_rope_w, k_rope_w[half:], k_rope_w[:half]], axis=0)
        uq = w_uq[l].reshape(Q_RANK, MLA_HEADS, QK_NOPE + QK_ROPE)
        uq_rope = uq[..., QK_NOPE:]
        wuq = jnp.concatenate([uq[..., :QK_NOPE], uq_rope, _half_swap(uq_rope)], axis=-1)
        wuq = wuq.reshape(Q_RANK, MLA_HEADS * QK_PAD).astype(BF16)
        bs_full = jnp.repeat(gmlp_b_s[l].T, GMLP_GROUP, axis=1)
        ukv = w_ukv[l].reshape(KV_RANK, MLA_HEADS, QK_NOPE + V_HEAD)
        wk = ukv[..., :QK_NOPE].reshape(KV_RANK, MLA_HEADS * QK_NOPE).astype(BF16)
        wvt = ukv[..., QK_NOPE:].reshape(KV_RANK, MLA_WIDTH).T.astype(BF16)
        a, q, k, vt = _mix_call(
            h, mod_rows, row(mix_norm[l]), win, wkpe, row(gmlp_v_norm[l]), gmlp_w_s[l].astype(BF16), bs_full,
            row(q_lat_norm[l]), wuq, row(kv_lat_norm[l]), wk, wvt,
            row(q_nope_norm[l]), row(jnp.concatenate([q_rope_norm[l], _half_swap(q_rope_norm[l])])),
            row(k_nope_norm[l]), row(jnp.concatenate([k_rope_norm[l], _half_swap(k_rope_norm[l])])),
            row(out_norm_gmlp[l]), pos, freq, seq=seq)

        m = _attn_call(q, k, vt, batch=batch, seq=seq)
        h = _outproj_call(h, a, m, mod_rows, row(out_norm_mla[l]), wout, seq=seq)

        h, _ = _ffn_call(h, mod_rows, 6, row(ffn2_norm[l]), row(final_norm[l]), wg2, wu2, wd2,
                         seq=seq, final_norm=True, tm=FFN_TM, tf=FFN_TF)
    return h.reshape(batch, seq, d)
```

```python
import functools
import math

import jax
import jax.numpy as jnp
from jax import lax
from jax.experimental import pallas as pl
from jax.experimental.pallas import tpu as pltpu

D_MODEL = 2048
GMLP_WIDTH = 1024
GMLP_GROUP = 128
GMLP_HEADS = GMLP_WIDTH // GMLP_GROUP
CHUNK = 128
V_HEAD = 128
MLA_WIDTH = 1024
MLA_HEADS = MLA_WIDTH // V_HEAD
QK_NOPE = 128
QK_ROPE = 64
Q_RANK = D_MODEL // 4
KV_RANK = D_MODEL // 8
IN_COLS = 2 * GMLP_WIDTH + Q_RANK + KV_RANK + QK_ROPE
D_FF = 5632
ROPE_THETA = 10000.0
EPS = 1e-6
N_MOD = 9
LOG2_E = 1.4426950408889634

LANES = 128
MXU_COLS = 256
QK_PAD = 2 * LANES
VMEM_LIMIT = 56 * 1024 * 1024
BF16_SUBLANES = 16
ATTN_SLOTS = 2
ATTN_SUB = 512
MIX_TM = 512
FFN_TM = 512
FFN_TF = 512
FFN_LEAD_TF = 256
FFN_LEAD_TM = 1024
FFN_K_SPLIT = 2
FFN_EDGE_SPLIT = 2

F32 = jnp.float32
BF16 = jnp.bfloat16


def _dot(a, b):
    return jnp.dot(a, b, preferred_element_type=F32)


def _dot_nt(a, b):
    return lax.dot_general(a, b, (((1,), (1,)), ((), ())), preferred_element_type=F32)


def _gelu_tanh(x):
    c0 = math.sqrt(2.0 / math.pi)
    hx = 0.5 * x
    return hx + hx * jnp.tanh(x * (c0 + (0.044715 * c0) * (x * x)))


def _rms_rows(x, gain):
    return x * lax.rsqrt(jnp.mean(x * x, axis=-1, keepdims=True) + EPS) * gain


def _const_spec(shape):
    zeros = (0,) * len(shape)
    return pl.BlockSpec(shape, lambda *_: zeros, pipeline_mode=pl.Buffered(1))


def _mod_kernel(c_ref, w_ref, b_ref, o_ref):
    c = c_ref[...]
    c_act = (c * jax.nn.sigmoid(c)).astype(BF16)
    o_ref[...] = _dot(c_act, w_ref[...].astype(BF16)) + b_ref[...]


def _mod_call(c_pad, w_ada, b_ada, *, tn=1024):
    rows, d = c_pad.shape
    n = w_ada.shape[1]
    return pl.pallas_call(
        _mod_kernel,
        grid=(n // tn,),
        in_specs=[
            pl.BlockSpec((rows, d), lambda j: (0, 0)),
            pl.BlockSpec((d, tn), lambda j: (0, j)),
            pl.BlockSpec((1, tn), lambda j: (0, j)),
        ],
        out_specs=pl.BlockSpec((rows, tn), lambda j: (0, j)),
        out_shape=jax.ShapeDtypeStruct((rows, n), F32),
        compiler_params=pltpu.CompilerParams(
            dimension_semantics=("arbitrary",), vmem_limit_bytes=VMEM_LIMIT),
        name="adaln_mod",
    )(c_pad, w_ada, b_ada)


def _ffn_kernel(*refs, final_norm, cast_every_step, lead, n_seed):
    seeded = n_seed > 0
    n_cast = len(cast_every_step)
    x_ref, sh_ref, sc_ref, g_ref, nw_ref, fin_ref, wg_ref, wu_ref, wd_ref = refs[:9]
    pos = 9
    seed_ref = refs[pos] if seeded else None
    pos += int(seeded)
    cast_in = refs[pos:pos + n_cast]
    pos += n_cast
    o_ref = refs[pos]
    pos += 1
    weight_out = refs[pos:pos + 3] if lead else None
    pos += 3 * int(lead)
    cast_out = refs[pos:pos + n_cast]
    n_scr, *acc = refs[pos + n_cast:]
    acc_ref = o_ref if lead else acc[0]
    i = pl.program_id(0)
    j = pl.program_id(1)
    last_j = pl.num_programs(1) - 1

    def cast(every_step):
        for src, dst, flag in zip(cast_in, cast_out, cast_every_step):
            if flag == every_step:
                dst[...] = src[...].astype(BF16)

    cast(True)

    @pl.when(j == 0)
    def _():
        cast(False)

    def step(first, last, n_split):
        if lead:
            wg, wu, wd = (w[...].astype(BF16) for w in (wg_ref, wu_ref, wd_ref))
            for dst, w in zip(weight_out, (wg, wu, wd)):
                dst[...] = w
        else:
            wg, wu, wd = wg_ref[...], wu_ref[...], wd_ref[...]
        sub = x_ref.shape[0] // n_split
        for r in range(n_split):
            rows = slice(r * sub, (r + 1) * sub)
            if first:
                gain = nw_ref[...] * (1.0 + sc_ref[...])
                n = (_rms_rows(x_ref[rows, :], gain) + sh_ref[...]).astype(BF16)
                n_scr[rows, :] = n
            else:
                n = n_scr[rows, :]
            y = None
            wide_enough = wg.shape[1] // FFN_K_SPLIT >= MXU_COLS
            k_split = FFN_K_SPLIT if (n_split == 1 and wide_enough) else 1
            half = wg.shape[1] // k_split
            for c in range(k_split):
                cols = slice(c * half, (c + 1) * half)
                hg = _dot(n, wg[:, cols])
                hu = _dot(n, wu[:, cols])
                hid = (hg * jax.nn.sigmoid(hg) * hu).astype(BF16)
                part = _dot(hid, wd[cols, :])
                y = part if y is None else y + part
            if not first:
                y = acc_ref[rows, :] + y
            if last:
                h = x_ref[rows, :] + (0.5 * g_ref[...]) * y
                if final_norm:
                    h = _rms_rows(h, fin_ref[...])
                o_ref[rows, :] = h
            else:
                acc_ref[rows, :] = y

    def when(cond):
        return pl.when(jnp.logical_and(cond, i >= n_seed) if seeded else cond)

    @when(j == 0)
    def _():
        step(True, False, FFN_EDGE_SPLIT)

    @when(jnp.logical_and(j > 0, j < last_j))
    def _():
        step(False, False, 1)

    @when(j == last_j)
    def _():
        step(False, True, FFN_EDGE_SPLIT)

    if seeded:
        @pl.when(jnp.logical_and(j == last_j, i < n_seed))
        def _():
            o_ref[...] = seed_ref[...]


def _ffn_call(x, mod_rows, mod_base, norm_w, fin_w, wg, wu, wd, *, seq, final_norm, tm, tf,
              cast_jobs=(), lead=False, seed=None):
    t, d = x.shape
    nf = wg.shape[1] // tf
    blocks_per_batch = seq // tm
    n_seed = 0 if seed is None else seed.shape[0] // tm
    seeded = n_seed > 0

    def row(k):
        return pl.BlockSpec((None, 1, d), lambda i, j: ((i // blocks_per_batch) * N_MOD + mod_base + k, 0, 0))

    def wj(i, j):
        return jnp.where(i < n_seed, 0, j) if seeded else j

    cast_arrays = [job[0] for job in cast_jobs]
    cast_specs = [pl.BlockSpec(job[1], job[2]) for job in cast_jobs]
    weight_specs = [
        pl.BlockSpec((d, tf), lambda i, j: (0, wj(i, j))),
        pl.BlockSpec((d, tf), lambda i, j: (0, wj(i, j))),
        pl.BlockSpec((tf, d), lambda i, j: (wj(i, j), 0)),
    ]
    n_rows = tm if lead else t
    outs = pl.pallas_call(
        functools.partial(_ffn_kernel, final_norm=final_norm, lead=lead, n_seed=n_seed,
                          cast_every_step=tuple(job[3] for job in cast_jobs)),
        grid=(n_rows // tm, nf),
        in_specs=[
            pl.BlockSpec((tm, d), lambda i, j: (i, 0), pipeline_mode=pl.Buffered(1 if lead else 2)),
            row(0), row(1), row(2),
            pl.BlockSpec((1, d), lambda i, j: (0, 0)),
            pl.BlockSpec((1, d), lambda i, j: (0, 0)),
            *weight_specs,
            *([pl.BlockSpec((tm, d), lambda i, j: (jnp.minimum(i, n_seed - 1), 0))] if seeded else []),
            *cast_specs,
        ],
        out_specs=[pl.BlockSpec((tm, d), lambda i, j: (i, 0)),
                   *(weight_specs if lead else []), *cast_specs],
        out_shape=[jax.ShapeDtypeStruct((n_rows, d), F32),
                   *([jax.ShapeDtypeStruct(w.shape, BF16) for w in (wg, wu, wd)] if lead else []),
                   *(jax.ShapeDtypeStruct(a.shape, BF16) for a in cast_arrays)],
        scratch_shapes=[pltpu.VMEM((tm, d), BF16), *([] if lead else [pltpu.VMEM((tm, d), F32)])],
        compiler_params=pltpu.CompilerParams(
            dimension_semantics=("parallel", "arbitrary"), vmem_limit_bytes=VMEM_LIMIT),
        name="swiglu_ffn_lead" if lead else "swiglu_ffn",
    )(x, mod_rows, mod_rows, mod_rows, norm_w, fin_w, wg, wu, wd,
      *([seed] if seeded else []), *cast_arrays)
    return outs[0], outs[1:]


def _mix_kernel(h_ref, sh_ref, sc_ref, nw_ref, win_ref, wkpe_ref, vn_ref, ws_ref, bs_ref,
                qln_ref, wuq_ref, kvn_ref, wk_ref, wvt_ref, qng_ref, qrg_ref, kng_ref, krg_ref,
                ong_ref, pos_ref, freq_ref,
                a_ref, q_ref, k_ref, vt_ref,
                u_scr, v_scr, a_scr, *, tm):
    o1 = GMLP_WIDTH
    o2 = 2 * GMLP_WIDTH
    o3 = o2 + Q_RANK
    o4 = o3 + KV_RANK
    scale = (QK_NOPE + QK_ROPE) ** -0.5 * LOG2_E

    gain = nw_ref[...] * (1.0 + sc_ref[...])
    n = (_rms_rows(h_ref[...], gain) + sh_ref[...]).astype(BF16)

    ang = pos_ref[...].astype(F32) * freq_ref[...]
    lane = lax.broadcasted_iota(jnp.int32, ang.shape, 1)
    sin = jnp.sin(ang)
    rot = jnp.where(lane < QK_ROPE, jnp.cos(ang), jnp.where(lane < QK_ROPE + QK_ROPE // 2, -sin, sin))

    def rope(z, gain):
        w = _rms_rows(z, gain) * rot
        return w + pltpu.roll(w, QK_ROPE, axis=1)

    q_lat = _rms_rows(_dot_nt(n, win_ref[o2:o3, :]), qln_ref[...]).astype(BF16)
    q_nope_gain = qng_ref[...] * scale
    q_rope_gain = qrg_ref[...] * scale
    for hd in range(MLA_HEADS):
        base = hd * QK_PAD
        qq = _dot(q_lat, wuq_ref[:, base:base + QK_PAD])
        q_ref[:, base:base + LANES] = _rms_rows(qq[:, :LANES], q_nope_gain).astype(BF16)
        q_ref[:, base + LANES:base + QK_PAD] = rope(qq[:, LANES:], q_rope_gain).astype(BF16)

    k_pe = rope(_dot_nt(n, wkpe_ref[...]), krg_ref[...])
    k_pe = jnp.where(lane < QK_ROPE, k_pe, 0.0).astype(BF16)
    kv_lat = _rms_rows(_dot_nt(n, win_ref[o3:o4, :]), kvn_ref[...]).astype(BF16)
    k_nope = _dot(kv_lat, wk_ref[...])
    for hd in range(MLA_HEADS):
        base = hd * QK_PAD
        cols = slice(hd * QK_NOPE, (hd + 1) * QK_NOPE)
        k_ref[:, base:base + LANES] = _rms_rows(k_nope[:, cols], kng_ref[...]).astype(BF16)
        k_ref[:, base + LANES:base + QK_PAD] = k_pe
    vt_ref[...] = _dot_nt(wvt_ref[...], kv_lat).astype(BF16)

    u_scr[...] = _gelu_tanh(_dot_nt(n, win_ref[:o1, :]))
    vv = _gelu_tanh(_dot_nt(n, win_ref[o1:o2, :]))
    v_scr[...] = _rms_rows(vv, vn_ref[...]).astype(BF16)
    for c in range(tm // CHUNK):
        rows = slice(c * CHUNK, (c + 1) * CHUNK)
        for g in range(GMLP_HEADS):
            cols = slice(g * GMLP_GROUP, (g + 1) * GMLP_GROUP)
            mixed = _dot(ws_ref[g], v_scr[rows, cols]) + bs_ref[:, cols]
            a_scr[rows, cols] = u_scr[rows, cols] * mixed
    a_ref[...] = _rms_rows(a_scr[...], ong_ref[...]).astype(BF16)


def _mix_call(h, mod_rows, mix_norm, win, wkpe, v_norm, ws, bs_full, q_lat_norm, wuq, kv_lat_norm, wk, wvt,
              qn_g, qr_g, kn_g, kr_g, on_g, pos, freq, *, seq, tm=MIX_TM):
    t, d = h.shape
    blocks_per_batch = seq // tm
    batch = t // seq

    def row(k):
        return pl.BlockSpec((None, 1, d), lambda i: ((i // blocks_per_batch) * N_MOD + 3 + k, 0, 0))

    def tok(width):
        return pl.BlockSpec((tm, width), lambda i: (i, 0))

    return pl.pallas_call(
        functools.partial(_mix_kernel, tm=tm),
        grid=(t // tm,),
        in_specs=[
            tok(d), row(0), row(1),
            _const_spec(mix_norm.shape), _const_spec(win.shape), _const_spec(wkpe.shape),
            _const_spec(v_norm.shape),
            _const_spec(ws.shape), _const_spec(bs_full.shape),
            _const_spec(q_lat_norm.shape), _const_spec(wuq.shape),
            _const_spec(kv_lat_norm.shape), _const_spec(wk.shape), _const_spec(wvt.shape),
            _const_spec(qn_g.shape), _const_spec(qr_g.shape), _const_spec(kn_g.shape),
            _const_spec(kr_g.shape), _const_spec(on_g.shape),
            tok(1), _const_spec(freq.shape),
        ],
        out_specs=[
            tok(GMLP_WIDTH), tok(MLA_HEADS * QK_PAD), tok(MLA_HEADS * QK_PAD),
            pl.BlockSpec((MLA_WIDTH, tm), lambda i: (i // blocks_per_batch, i % blocks_per_batch)),
        ],
        out_shape=[
            jax.ShapeDtypeStruct((t, GMLP_WIDTH), BF16),
            jax.ShapeDtypeStruct((t, MLA_HEADS * QK_PAD), BF16),
            jax.ShapeDtypeStruct((t, MLA_HEADS * QK_PAD), BF16),
            jax.ShapeDtypeStruct((batch * MLA_WIDTH, seq), BF16),
        ],
        scratch_shapes=[
            pltpu.VMEM((tm, GMLP_WIDTH), F32),
            pltpu.VMEM((tm, GMLP_WIDTH), BF16),
            pltpu.VMEM((tm, GMLP_WIDTH), F32),
        ],
        compiler_params=pltpu.CompilerParams(
            dimension_semantics=("parallel",), vmem_limit_bytes=VMEM_LIMIT),
        name="mixer_in",
    )(h, mod_rows, mod_rows, mix_norm, win, wkpe, v_norm, ws, bs_full, q_lat_norm, wuq, kv_lat_norm, wk, wvt,
      qn_g, qr_g, kn_g, kr_g, on_g, pos, freq)


def _attn_kernel(q_ref, k_ref, vt_ref, o_ref, vx_scr, s_scr, p_scr, *, sub, nsub):
    vx_scr[:V_HEAD, :] = vt_ref[...]
    vx_scr[V_HEAD:, :] = jnp.ones((vx_scr.shape[0] - V_HEAD, vx_scr.shape[1]), BF16)

    def rows(i):
        return slice(i * sub, (i + 1) * sub)

    def qk(i):
        s_scr[i % ATTN_SLOTS] = _dot_nt(k_ref[...], q_ref[rows(i), :])

    def softmax(i):
        s = s_scr[i % ATTN_SLOTS]
        p_scr[i % ATTN_SLOTS] = jnp.exp2(s - jnp.max(s, axis=0, keepdims=True)).astype(BF16)

    def pv(i):
        ox = _dot(vx_scr[...], p_scr[i % ATTN_SLOTS])
        o_ref[rows(i), :] = (ox[:V_HEAD, :] / ox[V_HEAD:V_HEAD + 1, :]).T

    qk(0)
    softmax(0)
    qk(1)
    for i in range(nsub - 2):
        pv(i)
        softmax(i + 1)
        qk(i + 2)
    pv(nsub - 2)
    softmax(nsub - 1)
    pv(nsub - 1)


def _attn_call(q, k, vt, *, batch, seq, sub=ATTN_SUB):
    t = q.shape[0]
    nsub = seq // sub
    assert nsub >= 3
    return pl.pallas_call(
        functools.partial(_attn_kernel, sub=sub, nsub=nsub),
        grid=(batch, MLA_HEADS),
        in_specs=[
            pl.BlockSpec((seq, QK_PAD), lambda b, h: (b, h)),
            pl.BlockSpec((seq, QK_PAD), lambda b, h: (b, h)),
            pl.BlockSpec((V_HEAD, seq), lambda b, h: (b * MLA_HEADS + h, 0)),
        ],
        out_specs=pl.BlockSpec((seq, V_HEAD), lambda b, h: (b, h)),
        out_shape=jax.ShapeDtypeStruct((t, MLA_WIDTH), F32),
        scratch_shapes=[
            pltpu.VMEM((V_HEAD + BF16_SUBLANES, seq), BF16),
            pltpu.VMEM((ATTN_SLOTS, seq, sub), F32),
            pltpu.VMEM((ATTN_SLOTS, seq, sub), BF16),
        ],
        compiler_params=pltpu.CompilerParams(
            dimension_semantics=("parallel", "parallel"), vmem_limit_bytes=VMEM_LIMIT),
        name="mla_attention",
    )(q, k, vt)


def _outproj_kernel(h_ref, a_ref, m_ref, g_ref, mg_ref, wo_ref, o_ref):
    m_n = _rms_rows(m_ref[...], mg_ref[...]).astype(BF16)
    y = _dot(a_ref[...], wo_ref[:GMLP_WIDTH, :]) + _dot(m_n, wo_ref[GMLP_WIDTH:, :])
    o_ref[...] = h_ref[...] + g_ref[...] * y


def _outproj_call(h, a, m, mod_rows, mla_gain, wo, *, seq, tm=MIX_TM):
    t, d = h.shape
    blocks_per_batch = seq // tm
    return pl.pallas_call(
        _outproj_kernel,
        grid=(t // tm,),
        in_specs=[
            pl.BlockSpec((tm, d), lambda i: (i, 0)),
            pl.BlockSpec((tm, GMLP_WIDTH), lambda i: (i, 0)),
            pl.BlockSpec((tm, MLA_WIDTH), lambda i: (i, 0)),
            pl.BlockSpec((None, 1, d), lambda i: ((i // blocks_per_batch) * N_MOD + 5, 0, 0)),
            _const_spec(mla_gain.shape), _const_spec(wo.shape),
        ],
        out_specs=pl.BlockSpec((tm, d), lambda i: (i, 0)),
        out_shape=jax.ShapeDtypeStruct((t, d), F32),
        compiler_params=pltpu.CompilerParams(
            dimension_semantics=("parallel",), vmem_limit_bytes=VMEM_LIMIT),
        name="mixer_out",
    )(h, a, m, mod_rows, mla_gain, wo)


def _half_swap(x):
    half = x.shape[-1] // 2
    return jnp.concatenate([x[..., half:], x[..., :half]], axis=-1)


def kernel(x, c, positions, w_ada, b_ada, ffn1_norm, ffn1_w_gate, ffn1_w_up, ffn1_w_down, mix_norm, w_in, gmlp_v_norm, gmlp_w_s, gmlp_b_s, q_lat_norm, w_uq, kv_lat_norm, w_ukv, q_nope_norm, q_rope_norm, k_nope_norm, k_rope_norm, out_norm_gmlp, out_norm_mla, w_out, ffn2_norm, ffn2_w_gate, ffn2_w_up, ffn2_w_down, final_norm):
    batch, seq, d = x.shape
    depth = w_ada.shape[0]
    t = batch * seq

    inv_freq = ROPE_THETA ** (-jnp.arange(0, QK_ROPE, 2, dtype=F32) / QK_ROPE)
    freq = jnp.tile(inv_freq, 4)[None, :]
    pos = positions.reshape(t, 1)
    c_pad = jnp.pad(c, ((0, 8 - batch), (0, 0)))

    def row(v):
        return v[None, :]

    h = x.reshape(t, d)
    for l in range(depth):
        mod = _mod_call(c_pad, w_ada[l], row(b_ada[l]))
        mod_rows = mod[:batch].reshape(batch * N_MOD, 1, d)

        w_in_t = w_in[l].T
        ni = t // FFN_TM
        jobs = [
            (ffn2_w_gate[l], (d // ni, FFN_TF), lambda i, j: (i, j), True),
            (ffn2_w_up[l], (d // ni, FFN_TF), lambda i, j: (i, j), True),
            (ffn2_w_down[l], (FFN_TF, d // ni), lambda i, j: (j, i), True),
            (w_in_t, (IN_COLS, d // ni), lambda i, j: (0, i), False),
            (w_out[l], (d // ni, d), lambda i, j: (i, 0), False),
        ]
        h_lead, ffn1_weights = _ffn_call(
            h, mod_rows, 0, row(ffn1_norm[l]), row(final_norm[l]),
            ffn1_w_gate[l], ffn1_w_up[l], ffn1_w_down[l],
            seq=seq, final_norm=False, tm=FFN_LEAD_TM, tf=FFN_LEAD_TF, lead=True)
        h, (wg2, wu2, wd2, win, wout) = _ffn_call(
            h, mod_rows, 0, row(ffn1_norm[l]), row(final_norm[l]), *ffn1_weights,
            seq=seq, final_norm=False, cast_jobs=jobs, seed=h_lead, tm=FFN_TM, tf=FFN_TF)

        k_rope_w = win[IN_COLS - QK_ROPE:]
        half = QK_ROPE // 2
        wkpe = jnp.concatenate([k_rope_w, k_rope_w[half:], k_rope_w[:half]], axis=0)
        uq = w_uq[l].reshape(Q_RANK, MLA_HEADS, QK_NOPE + QK_ROPE)
        uq_rope = uq[..., QK_NOPE:]
        wuq = jnp.concatenate([uq[..., :QK_NOPE], uq_rope, _half_swap(uq_rope)], axis=-1)
        wuq = wuq.reshape(Q_RANK, MLA_HEADS * QK_PAD).astype(BF16)
        bs_full = jnp.repeat(gmlp_b_s[l].T, GMLP_GROUP, axis=1)
        ukv = w_ukv[l].reshape(KV_RANK, MLA_HEADS, QK_NOPE + V_HEAD)
        wk = ukv[..., :QK_NOPE].reshape(KV_RANK, MLA_HEADS * QK_NOPE).astype(BF16)
        wvt = ukv[..., QK_NOPE:].reshape(KV_RANK, MLA_WIDTH).T.astype(BF16)
        a, q, k, vt = _mix_call(
            h, mod_rows, row(mix_norm[l]), win, wkpe, row(gmlp_v_norm[l]), gmlp_w_s[l].astype(BF16), bs_full,
            row(q_lat_norm[l]), wuq, row(kv_lat_norm[l]), wk, wvt,
            row(q_nope_norm[l]), row(jnp.concatenate([q_rope_norm[l], _half_swap(q_rope_norm[l])])),
            row(k_nope_norm[l]), row(jnp.concatenate([k_rope_norm[l], _half_swap(k_rope_norm[l])])),
            row(out_norm_gmlp[l]), pos, freq, seq=seq)

        m = _attn_call(q, k, vt, batch=batch, seq=seq)
        h = _outproj_call(h, a, m, mod_rows, row(out_norm_mla[l]), wout, seq=seq)

        h, _ = _ffn_call(h, mod_rows, 6, row(ffn2_norm[l]), row(final_norm[l]), wg2, wu2, wd2,
                         seq=seq, final_norm=True, tm=FFN_TM, tf=FFN_TF)
    return h.reshape(batch, seq, d)
```
